```python
import math
import jax, jax.numpy as jnp
from jax import lax
import numpy as np

D_MODEL = 1024
BATCH = 8
SEQ = 8192
DEPTH = 1

RET_HEADS = 4
RET_DK = 64
RET_DV = 128
RET_QK = RET_HEADS * RET_DK
RET_V = RET_HEADS * RET_DV
CHUNK = 128
FOX_HEADS = 8
FOX_DH = 64
FOX_W = FOX_HEADS * FOX_DH
Q_BLOCK = 128
D_FF = -(-8 * D_MODEL // (3 * 256)) * 256
ROPE_BASE = 10000.0
EPS = 1e-6
IN_SIZES = (RET_QK, RET_QK, RET_V, RET_V, FOX_W, FOX_W, FOX_W, FOX_HEADS, D_MODEL, D_MODEL)
IN_COLS = sum(IN_SIZES)

kernel_name = "hybrid_retention_fox_gated_block"


def rmsnorm(x, g):
    xf = x.astype(jnp.float32)
    y = xf * lax.rsqrt(jnp.mean(xf * xf, axis=-1, keepdims=True) + EPS)
    return (y * g.astype(jnp.float32)).astype(x.dtype)


def rotary(x, pos):
    half = x.shape[-1] // 2
    inv_freq = 1.0 / (ROPE_BASE ** (jnp.arange(half, dtype=jnp.float32) / half))
    ang = pos[:, None] * inv_freq[None, :]
    cos = jnp.cos(ang)[None, :, None, :]
    sin = jnp.sin(ang)[None, :, None, :]
    xf = x.astype(jnp.float32)
    x1, x2 = xf[..., :half], xf[..., half:]
    return jnp.concatenate([x1 * cos - x2 * sin, x1 * sin + x2 * cos], axis=-1)


def retention_chunkwise(q, k, v):
    B, S, H, dk = q.shape
    dv = v.shape[-1]
    n = S // CHUNK
    log_g = jnp.log1p(-(2.0 ** (-5.0 - jnp.arange(H, dtype=jnp.float32))))
    qc = q.astype(jnp.float32).reshape(B, n, CHUNK, H, dk)
    kc = k.astype(jnp.float32).reshape(B, n, CHUNK, H, dk)
    vc = v.astype(jnp.float32).reshape(B, n, CHUNK, H, dv)
    idx = jnp.arange(CHUNK, dtype=jnp.float32)
    diff = idx[:, None] - idx[None, :]
    decay = jnp.where(diff[None] >= 0, jnp.exp(jnp.maximum(diff, 0.0)[None] * log_g[:, None, None]), 0.0)
    scores = jnp.einsum('bnihd,bnjhd->bnhij', qc, kc) * decay[None, None]
    intra = jnp.einsum('bnhij,bnjhe->bnihe', scores, vc)
    zeta = jnp.exp((CHUNK - 1.0 - idx)[None, :] * log_g[:, None])
    chunk_kv = jnp.einsum('bnjhd,hj,bnjhe->bnhde', kc, zeta, vc)
    g_chunk = jnp.exp(CHUNK * log_g)[:, None, None]

    def step(r_prev, kv):
        return g_chunk * r_prev + kv, r_prev

    r0 = jnp.zeros((B, H, dk, dv), jnp.float32)
    _, states = lax.scan(step, r0, jnp.moveaxis(chunk_kv, 1, 0))
    states = jnp.moveaxis(states, 0, 1)
    xi = jnp.exp((idx + 1.0)[None, :] * log_g[:, None]).T
    inter = jnp.einsum('bnihd,bnhde->bnihe', qc, states) * xi[None, None, :, :, None]
    return (intra + inter).reshape(B, S, H, dv)


def forgetting_attention(q, k, v, log_f):
    B, S, H, d = q.shape
    nb = S // Q_BLOCK
    scale = 1.0 / math.sqrt(d)
    c = jnp.cumsum(log_f, axis=1).transpose(0, 2, 1)
    qh = q.transpose(0, 2, 1, 3)
    kh = k.transpose(0, 2, 1, 3)
    vh = v.transpose(0, 2, 1, 3)
    qb = qh.reshape(B, H, nb, Q_BLOCK, d).transpose(2, 0, 1, 3, 4)
    cb = c.reshape(B, H, nb, Q_BLOCK).transpose(2, 0, 1, 3)
    pos_k = jnp.arange(S)

    def block(args):
        i, q_blk, c_blk = args
        s = jnp.einsum('bhqd,bhkd->bhqk', q_blk, kh).astype(jnp.float32) * scale
        s = s + c_blk[..., None] - c[:, :, None, :]
        pos_q = i * Q_BLOCK + jnp.arange(Q_BLOCK)
        mask = pos_k[None, :] <= pos_q[:, None]
        s = jnp.where(mask[None, None], s, -jnp.inf)
        p = jax.nn.softmax(s, axis=-1)
        return jnp.einsum('bhqk,bhkd->bhqd', p.astype(vh.dtype), vh)

    out = lax.map(block, (jnp.arange(nb), qb, cb))
    return out.transpose(1, 2, 0, 3, 4).reshape(B, H, S, d).transpose(0, 2, 1, 3)


def split_cols(z):
    offs = np.cumsum(np.array(IN_SIZES))[:-1].tolist()
    return jnp.split(z, offs, axis=-1)


def setup_inputs(seed: int = 0) -> dict:
    key = jax.random.key(seed)
    ks = jax.random.split(key, 16)
    f32 = jnp.float32
    nrm = lambda k, shp: jax.random.normal(k, shp, f32)
    L = DEPTH
    return {
        "x": nrm(ks[0], (BATCH, SEQ, D_MODEL)),
        "g_mix": 1.0 + 0.02 * nrm(ks[1], (L, D_MODEL)),
        "w_in": nrm(ks[2], (L, D_MODEL, IN_COLS)) * D_MODEL ** -0.5,
        "b_forget": 1.0 + 0.5 * nrm(ks[3], (L, FOX_HEADS)),
        "g_ret_norm": 1.0 + 0.02 * nrm(ks[4], (L, RET_V)),
        "w_ret_o": nrm(ks[5], (L, RET_V, D_MODEL)) * RET_V ** -0.5,
        "g_fox_q": 1.0 + 0.02 * nrm(ks[6], (L, FOX_DH)),
        "g_fox_k": 1.0 + 0.02 * nrm(ks[7], (L, FOX_DH)),
        "w_fox_o": nrm(ks[8], (L, FOX_W, D_MODEL)) * FOX_W ** -0.5,
        "w_out": nrm(ks[9], (L, D_MODEL, D_MODEL)) * D_MODEL ** -0.5,
        "g_ffn": 1.0 + 0.02 * nrm(ks[10], (L, D_MODEL)),
        "w_gate": nrm(ks[11], (L, D_MODEL, D_FF)) * D_MODEL ** -0.5,
        "w_up": nrm(ks[12], (L, D_MODEL, D_FF)) * D_MODEL ** -0.5,
        "w_down": nrm(ks[13], (L, D_FF, D_MODEL)) * D_FF ** -0.5,
    }


def reference(x, g_mix, w_in, b_forget, g_ret_norm, w_ret_o, g_fox_q, g_fox_k, w_fox_o,
              w_out, g_ffn, w_gate, w_up, w_down):
    B, S, _ = x.shape
    pos = jnp.arange(S, dtype=jnp.float32)
    for l in range(DEPTH):
        h = rmsnorm(x, g_mix[l])
        z = h @ w_in[l]
        q_r, k_r, v_r, gt_r, q_f, k_f, v_f, f_f, a_r, a_f = split_cols(z)

        q_r = rotary(q_r.reshape(B, S, RET_HEADS, RET_DK), pos)
        k_r = rotary(k_r.reshape(B, S, RET_HEADS, RET_DK), pos) * (RET_DK ** -0.5)
        v_r = v_r.reshape(B, S, RET_HEADS, RET_DV)
        o_r = retention_chunkwise(q_r, k_r, v_r)
        mu = jnp.mean(o_r, axis=-1, keepdims=True)
        var = jnp.mean(jnp.square(o_r - mu), axis=-1, keepdims=True)
        o_r = ((o_r - mu) * lax.rsqrt(var + EPS)).reshape(B, S, RET_V) * g_ret_norm[l].astype(jnp.float32)
        o_r = (jax.nn.silu(gt_r.astype(jnp.float32)) * o_r).astype(x.dtype)
        y_r = o_r @ w_ret_o[l]

        q_f = rmsnorm(q_f.reshape(B, S, FOX_HEADS, FOX_DH), g_fox_q[l])
        k_f = rmsnorm(k_f.reshape(B, S, FOX_HEADS, FOX_DH), g_fox_k[l])
        v_f = v_f.reshape(B, S, FOX_HEADS, FOX_DH)
        log_f = jax.nn.log_sigmoid(f_f.astype(jnp.float32) + b_forget[l].astype(jnp.float32))
        o_f = forgetting_attention(q_f, k_f, v_f, log_f).reshape(B, S, FOX_W).astype(x.dtype)
        y_f = o_f @ w_fox_o[l]

        merged = (jax.nn.sigmoid(a_r.astype(jnp.float32)) * y_r.astype(jnp.float32)
                  + jax.nn.sigmoid(a_f.astype(jnp.float32)) * y_f.astype(jnp.float32)).astype(x.dtype)
        x = x + merged @ w_out[l]

        h2 = rmsnorm(x, g_ffn[l])
        ff = (jax.nn.silu(h2 @ w_gate[l]) * (h2 @ w_up[l])) @ w_down[l]
        x = x + ff
    return x
```

```python
import functools
import math

import jax
import jax.numpy as jnp
from jax import lax
from jax.experimental import pallas as pl
from jax.experimental.pallas import tpu as pltpu

F32 = jnp.float32
BF16 = jnp.bfloat16

D_MODEL = 1024
RET_HEADS = 4
RET_DK = 64
RET_DV = 128
RET_QK = RET_HEADS * RET_DK
RET_V = RET_HEADS * RET_DV
CHUNK = 128
FOX_HEADS = 8
FOX_DH = 64
FOX_W = FOX_HEADS * FOX_DH
D_FF = 2816
ROPE_BASE = 10000.0
EPS = 1e-6

Z_QR, Z_KR, Z_VR, Z_GR = 0, 256, 512, 1024
Z_QF, Z_KF, Z_VF, Z_FF = 1536, 2048, 2560, 3072
Z_ROWS = 3088
AUG = 16
FOX_ROWS = FOX_DH + AUG
NEG = -1e30

TM_IN = 512
TS_RET = 1024
TQ = 256
TK = 256
TM_MERGE = 512
TM_FFN = 512
VMEM_LIMIT = 56 * 1024 * 1024

NT_DIMS = (((1,), (1,)), ((), ()))
TN_DIMS = (((0,), (0,)), ((), ()))


def _split3(c):
    hi = c.astype(BF16).astype(F32)
    r = c - hi
    mid = r.astype(BF16).astype(F32)
    lo = (r - mid).astype(BF16).astype(F32)
    return hi, mid, lo


def _inproj_kernel(x_ref, gmix_ref, wt_ref, cos_ref, sin_ref, gq_ref, gk_ref, bf_ref,
                   qr_ref, kr_ref, vr_ref, gr_ref, qf_ref, kf_ref, vf_ref,
                   carry_ref):
    tm = x_ref.shape[1]

    @pl.when(pl.program_id(1) == 0)
    def _():
        carry_ref[...] = jnp.zeros_like(carry_ref)

    x = x_ref[0]
    ms = jnp.mean(x * x, axis=-1, keepdims=True)
    h = (x * lax.rsqrt(ms + EPS) * gmix_ref[...]).astype(BF16)
    z = lax.dot_general(wt_ref[...], h, NT_DIMS, preferred_element_type=F32)

    cos = cos_ref[...]
    sin = sin_ref[...]
    half = RET_DK // 2
    for hh in range(RET_HEADS):
        for base, ref, scl in ((Z_QR, qr_ref, 1.0), (Z_KR, kr_ref, RET_DK ** -0.5)):
            r0 = base + hh * RET_DK
            x1 = z[r0:r0 + half]
            x2 = z[r0 + half:r0 + RET_DK]
            o1 = (x1 * cos - x2 * sin) * scl
            o2 = (x1 * sin + x2 * cos) * scl
            ref[0, hh * RET_DK:hh * RET_DK + half, :] = o1.astype(BF16)
            ref[0, hh * RET_DK + half:(hh + 1) * RET_DK, :] = o2.astype(BF16)

    vr_ref[0] = z[Z_VR:Z_VR + RET_V].astype(BF16)
    g = z[Z_GR:Z_GR + RET_V]
    gr_ref[0] = (g * jax.nn.sigmoid(g)).astype(BF16)

    u = z[Z_FF:Z_FF + FOX_HEADS] + bf_ref[...]
    ls = jnp.minimum(u, 0.0) - jnp.log1p(jnp.exp(-jnp.abs(u)))
    lane = lax.broadcasted_iota(jnp.int32, ls.shape, 1)
    c = ls
    d = 1
    while d < tm:
        c = c + jnp.where(lane >= d, pltpu.roll(c, d, axis=1), 0.0)
        d *= 2
    c = c + carry_ref[:, 0:1]
    carry_ref[...] = jnp.broadcast_to(c[:, tm - 1:tm], carry_ref.shape)

    ridx = lax.broadcasted_iota(jnp.int32, (AUG, tm), 0)
    scale = FOX_DH ** -0.5
    for hh in range(FOX_HEADS):
        r0 = hh * FOX_DH
        q = z[Z_QF + r0:Z_QF + r0 + FOX_DH]
        k = z[Z_KF + r0:Z_KF + r0 + FOX_DH]
        qn = q * lax.rsqrt(jnp.mean(q * q, axis=0, keepdims=True) + EPS) * gq_ref[...]
        kn = k * lax.rsqrt(jnp.mean(k * k, axis=0, keepdims=True) + EPS) * gk_ref[...]
        c_hi, c_mid, c_lo = _split3(c[hh:hh + 1, :])
        one = jnp.where(ridx < 6, 1.0, 0.0)
        q_aug = jnp.where(ridx == 0, c_hi, jnp.where(ridx == 1, c_mid,
                          jnp.where(ridx == 2, c_lo, jnp.where(ridx < 6, 1.0, 0.0))))
        k_aug = jnp.where(ridx == 3, -c_hi, jnp.where(ridx == 4, -c_mid,
                          jnp.where(ridx == 5, -c_lo, jnp.where(ridx < 3, 1.0, 0.0))))
        del one
        qf_ref[0, hh, 0:FOX_DH, :] = (qn * scale).astype(BF16)
        qf_ref[0, hh, FOX_DH:FOX_ROWS, :] = q_aug.astype(BF16)
        kf_ref[0, hh, 0:FOX_DH, :] = kn.astype(BF16)
        kf_ref[0, hh, FOX_DH:FOX_ROWS, :] = k_aug.astype(BF16)
        vf_ref[0, hh, 0:FOX_DH, :] = z[Z_VF + r0:Z_VF + r0 + FOX_DH].astype(BF16)
        vf_ref[0, hh, FOX_DH:FOX_ROWS, :] = jnp.where(ridx == 0, 1.0, 0.0).astype(BF16)


def _inproj(x, gmix, wt, cos_t, sin_t, gq, gk, bf):
    B, S, D = x.shape
    tm = min(TM_IN, S)
    grid = (B, S // tm)
    const = lambda b, s: (0, 0)
    fm = lambda rows: pl.BlockSpec((1, rows, tm), lambda b, s: (b, 0, s))
    fox = pl.BlockSpec((1, FOX_HEADS, FOX_ROWS, tm), lambda b, s: (b, 0, 0, s))
    out_shape = (
        jax.ShapeDtypeStruct((B, RET_QK, S), BF16),
        jax.ShapeDtypeStruct((B, RET_QK, S), BF16),
        jax.ShapeDtypeStruct((B, RET_V, S), BF16),
        jax.ShapeDtypeStruct((B, RET_V, S), BF16),
        jax.ShapeDtypeStruct((B, FOX_HEADS, FOX_ROWS, S), BF16),
        jax.ShapeDtypeStruct((B, FOX_HEADS, FOX_ROWS, S), BF16),
        jax.ShapeDtypeStruct((B, FOX_HEADS, FOX_ROWS, S), BF16),
    )
    return pl.pallas_call(
        _inproj_kernel,
        grid=grid,
        in_specs=[
            pl.BlockSpec((1, tm, D), lambda b, s: (b, s, 0)),
            pl.BlockSpec((1, D), const),
            pl.BlockSpec((Z_ROWS, D), const, pipeline_mode=pl.Buffered(1)),
            pl.BlockSpec((RET_DK // 2, tm), lambda b, s: (0, s)),
            pl.BlockSpec((RET_DK // 2, tm), lambda b, s: (0, s)),
            pl.BlockSpec((FOX_DH, 1), const),
            pl.BlockSpec((FOX_DH, 1), const),
            pl.BlockSpec((FOX_HEADS, 1), const),
        ],
        out_specs=(fm(RET_QK), fm(RET_QK), fm(RET_V), fm(RET_V), fox, fox, fox),
        out_shape=out_shape,
        scratch_shapes=[pltpu.VMEM((FOX_HEADS, 128), F32)],
        compiler_params=pltpu.CompilerParams(
            dimension_semantics=("arbitrary", "arbitrary"),
            vmem_limit_bytes=VMEM_LIMIT),
        name="inproj",
    )(x, gmix, wt, cos_t, sin_t, gq, gk, bf)


def _retention_kernel(q_ref, k_ref, v_ref, g_ref, decay_ref, zeta_ref, xi_ref, gch_ref, gn_ref,
                      o_ref, st_ref):
    ts = q_ref.shape[2]

    @pl.when(pl.program_id(1) == 0)
    def _():
        st_ref[...] = jnp.zeros_like(st_ref)

    def chunk_body(ci, _):
        t0 = pl.multiple_of(ci * CHUNK, CHUNK)
        for hh in range(RET_HEADS):
            qT = q_ref[0, hh * RET_DK:(hh + 1) * RET_DK, pl.ds(t0, CHUNK)]
            kT = k_ref[0, hh * RET_DK:(hh + 1) * RET_DK, pl.ds(t0, CHUNK)]
            vT = v_ref[0, hh * RET_DV:(hh + 1) * RET_DV, pl.ds(t0, CHUNK)]
            sg = g_ref[0, hh * RET_DV:(hh + 1) * RET_DV, pl.ds(t0, CHUNK)]
            sT = lax.dot_general(kT, qT, TN_DIMS, preferred_element_type=F32)
            pT = (sT * decay_ref[hh]).astype(BF16)
            intra = jnp.dot(vT, pT, preferred_element_type=F32)
            st = st_ref[hh]
            inter = lax.dot_general(st.astype(BF16), qT, TN_DIMS,
                                    preferred_element_type=F32)
            o = intra + inter * xi_ref[hh]
            vz = (vT.astype(F32) * zeta_ref[hh]).astype(BF16)
            kv = lax.dot_general(kT, vz, NT_DIMS, preferred_element_type=F32)
            st_ref[hh] = gch_ref[hh] * st + kv
            mu = jnp.mean(o, axis=0, keepdims=True)
            oc = o - mu
            var = jnp.mean(oc * oc, axis=0, keepdims=True)
            on = oc * lax.rsqrt(var + EPS) * gn_ref[hh]
            o_ref[0, hh * RET_DV:(hh + 1) * RET_DV, pl.ds(t0, CHUNK)] = (
                sg.astype(F32) * on).astype(BF16)
        return 0

    lax.fori_loop(0, ts // CHUNK, chunk_body, 0)


def _retention(qr, kr, vr, gr, decay_t, zeta, xi, gch, gn):
    B, _, S = qr.shape
    ts = min(TS_RET, S)
    grid = (B, S // ts)
    fm = lambda rows: pl.BlockSpec((1, rows, ts), lambda b, s: (b, 0, s))
    c3 = lambda b, s: (0, 0, 0)
    return pl.pallas_call(
        _retention_kernel,
        grid=grid,
        in_specs=[
            fm(RET_QK), fm(RET_QK), fm(RET_V), fm(RET_V),
            pl.BlockSpec((RET_HEADS, CHUNK, CHUNK), c3),
            pl.BlockSpec((RET_HEADS, 1, CHUNK), c3),
            pl.BlockSpec((RET_HEADS, 1, CHUNK), c3),
            pl.BlockSpec((RET_HEADS, 1, 1), c3),
            pl.BlockSpec((RET_HEADS, RET_DV, 1), c3),
        ],
        out_specs=fm(RET_V),
        out_shape=jax.ShapeDtypeStruct((B, RET_V, S), BF16),
        scratch_shapes=[pltpu.VMEM((RET_HEADS, RET_DK, RET_DV), F32)],
        compiler_params=pltpu.CompilerParams(
            dimension_semantics=("arbitrary", "arbitrary"),
            vmem_limit_bytes=VMEM_LIMIT),
        name="retention",
    )(qr, kr, vr, gr, decay_t, zeta, xi, gch, gn)


def _fox_kernel(q_ref, k_ref, v_ref, o_ref):
    S = q_ref.shape[3]
    tq = min(TQ, S)
    tk = tq
    row = lax.broadcasted_iota(jnp.int32, (tk, tq), 0)
    col = lax.broadcasted_iota(jnp.int32, (tk, tq), 1)
    future = row > col

    def q_body(qi, _):
        q0 = pl.multiple_of(qi * tq, tq)
        qT = q_ref[0, 0, :, pl.ds(q0, tq)]

        def kv_step(j, carry, masked):
            m, acc = carry
            k0 = pl.multiple_of(j * tk, tk)
            kT = k_ref[0, 0, :, pl.ds(k0, tk)]
            vT = v_ref[0, 0, :, pl.ds(k0, tk)]
            s = lax.dot_general(kT, qT, TN_DIMS, preferred_element_type=F32)
            if masked:
                s = jnp.where(future, NEG, s)
            m_new = jnp.maximum(m, jnp.max(s, axis=0, keepdims=True))
            p = jnp.exp(s - m_new)
            alpha = jnp.exp(m - m_new)
            acc = acc * alpha + jnp.dot(vT, p.astype(BF16), preferred_element_type=F32)
            return m_new, acc

        m0 = jnp.full((1, tq), NEG, F32)
        acc0 = jnp.zeros((FOX_ROWS, tq), F32)
        m, acc = lax.fori_loop(0, qi, lambda j, c: kv_step(j, c, False), (m0, acc0))
        m, acc = kv_step(qi, (m, acc), True)
        o = acc[0:FOX_DH] / acc[FOX_DH:FOX_DH + 1]
        o_ref[0, :, pl.ds(q0, tq)] = o.astype(BF16)
        return 0

    lax.fori_loop(0, S // tq, q_body, 0)


def _fox(qf, kf, vf):
    B, H, R, S = qf.shape
    spec = pl.BlockSpec((1, 1, R, S), lambda b, h: (b, h, 0, 0))
    return pl.pallas_call(
        _fox_kernel,
        grid=(B, H),
        in_specs=[spec, spec, spec],
        out_specs=pl.BlockSpec((1, FOX_DH, S), lambda b, h: (b, h, 0)),
        out_shape=jax.ShapeDtypeStruct((B, FOX_W, S), BF16),
        compiler_params=pltpu.CompilerParams(
            dimension_semantics=("arbitrary", "arbitrary"),
            vmem_limit_bytes=VMEM_LIMIT),
        name="fox",
    )(qf, kf, vf)


def _merge_kernel(x_ref, or_ref, of_ref, gmix_ref, wa_ref, wro_ref, wfo_ref, wout_ref, o_ref):
    x = x_ref[0]
    ms = jnp.mean(x * x, axis=-1, keepdims=True)
    h = (x * lax.rsqrt(ms + EPS) * gmix_ref[...]).astype(BF16)
    a = jnp.dot(h, wa_ref[...], preferred_element_type=F32)
    y_r = lax.dot_general(or_ref[0], wro_ref[...], TN_DIMS, preferred_element_type=F32)
    y_f = lax.dot_general(of_ref[0], wfo_ref[...], TN_DIMS, preferred_element_type=F32)
    merged = (jax.nn.sigmoid(a[:, :D_MODEL]) * y_r
              + jax.nn.sigmoid(a[:, D_MODEL:]) * y_f).astype(BF16)
    o_ref[0] = x + jnp.dot(merged, wout_ref[...], preferred_element_type=F32)


def _merge(x, o_r, o_f, gmix, wa, wro, wfo, wout):
    B, S, D = x.shape
    tm = min(TM_MERGE, S)
    const = lambda b, s: (0, 0)
    w = lambda shape: pl.BlockSpec(shape, const, pipeline_mode=pl.Buffered(1))
    return pl.pallas_call(
        _merge_kernel,
        grid=(B, S // tm),
        in_specs=[
            pl.BlockSpec((1, tm, D), lambda b, s: (b, s, 0)),
            pl.BlockSpec((1, RET_V, tm), lambda b, s: (b, 0, s)),
            pl.BlockSpec((1, FOX_W, tm), lambda b, s: (b, 0, s)),
            pl.BlockSpec((1, D), const),
            w((D, 2 * D)), w((RET_V, D)), w((FOX_W, D)), w((D, D)),
        ],
        out_specs=pl.BlockSpec((1, tm, D), lambda b, s: (b, s, 0)),
        out_shape=jax.ShapeDtypeStruct((B, S, D), F32),
        compiler_params=pltpu.CompilerParams(
            dimension_semantics=("arbitrary", "arbitrary"),
            vmem_limit_bytes=VMEM_LIMIT),
        name="merge",
    )(x, o_r, o_f, gmix, wa, wro, wfo, wout)


def _ffn_kernel(x_ref, g_ref, wg_ref, wu_ref, wd_ref, o_ref):
    x = x_ref[0]
    ms = jnp.mean(x * x, axis=-1, keepdims=True)
    h = (x * lax.rsqrt(ms + EPS) * g_ref[...]).astype(BF16)
    gate = jnp.dot(h, wg_ref[...], preferred_element_type=F32)
    up = jnp.dot(h, wu_ref[...], preferred_element_type=F32)
    act = (gate * jax.nn.sigmoid(gate) * up).astype(BF16)
    o_ref[0] = x + jnp.dot(act, wd_ref[...], preferred_element_type=F32)


def _ffn(x, g, wg, wu, wd):
    B, S, D = x.shape
    tm = min(TM_FFN, S)
    const = lambda b, s: (0, 0)
    w = lambda shape: pl.BlockSpec(shape, const, pipeline_mode=pl.Buffered(1))
    return pl.pallas_call(
        _ffn_kernel,
        grid=(B, S // tm),
        in_specs=[
            pl.BlockSpec((1, tm, D), lambda b, s: (b, s, 0)),
            pl.BlockSpec((1, D), const),
            w((D, D_FF)), w((D, D_FF)), w((D_FF, D)),
        ],
        out_specs=pl.BlockSpec((1, tm, D), lambda b, s: (b, s, 0)),
        out_shape=jax.ShapeDtypeStruct((B, S, D), F32),
        compiler_params=pltpu.CompilerParams(
            dimension_semantics=("arbitrary", "arbitrary"),
            vmem_limit_bytes=VMEM_LIMIT),
        name="ffn",
    )(x, g, wg, wu, wd)


def _rope_tables(S):
    half = RET_DK // 2
    pos = jnp.arange(S, dtype=F32)
    inv_freq = 1.0 / (ROPE_BASE ** (jnp.arange(half, dtype=F32) / half))
    ang = pos[:, None] * inv_freq[None, :]
    return jnp.cos(ang).T, jnp.sin(ang).T


def _retention_tables():
    log_g = jnp.log1p(-(2.0 ** (-5.0 - jnp.arange(RET_HEADS, dtype=F32))))
    idx = jnp.arange(CHUNK, dtype=F32)
    diff = idx[:, None] - idx[None, :]
    decay = jnp.where(diff[None] >= 0,
                      jnp.exp(jnp.maximum(diff, 0.0)[None] * log_g[:, None, None]), 0.0)
    decay_t = jnp.swapaxes(decay, 1, 2)
    zeta = jnp.exp((CHUNK - 1.0 - idx)[None, :] * log_g[:, None])[:, None, :]
    xi = jnp.exp((idx + 1.0)[None, :] * log_g[:, None])[:, None, :]
    gch = jnp.exp(CHUNK * log_g)[:, None, None]
    return decay_t, zeta, xi, gch


def kernel(x, g_mix, w_in, b_forget, g_ret_norm, w_ret_o, g_fox_q, g_fox_k, w_fox_o,
           w_out, g_ffn, w_gate, w_up, w_down):
    B, S, D = x.shape
    depth = g_mix.shape[0]
    cos_t, sin_t = _rope_tables(S)
    decay_t, zeta, xi, gch = _retention_tables()
    n_z = Z_FF + FOX_HEADS
    for l in range(depth):
        wt = jnp.pad(w_in[l][:, :n_z].T, ((0, Z_ROWS - n_z), (0, 0))).astype(BF16)
        wa = w_in[l][:, n_z:].astype(BF16)
        gmix = g_mix[l][None, :]
        qr, kr, vr, gr, qf, kf, vf = _inproj(
            x, gmix, wt, cos_t, sin_t,
            g_fox_q[l][:, None], g_fox_k[l][:, None], b_forget[l][:, None])
        o_r = _retention(qr, kr, vr, gr, decay_t, zeta, xi, gch,
                         g_ret_norm[l].reshape(RET_HEADS, RET_DV, 1))
        o_f = _fox(qf, kf, vf)
        x = _merge(x, o_r, o_f, gmix, wa, w_ret_o[l].astype(BF16), w_fox_o[l].astype(BF16),
                   w_out[l].astype(BF16))
        x = _ffn(x, g_ffn[l][None, :], w_gate[l].astype(BF16), w_up[l].astype(BF16),
                 w_down[l].astype(BF16))
    return x
```

```python
import functools
import math

import jax
import jax.numpy as jnp
from jax import lax
from jax.experimental import pallas as pl
from jax.experimental.pallas import tpu as pltpu

F32 = jnp.float32
BF16 = jnp.bfloat16

D_MODEL = 1024
RET_HEADS = 4
RET_DK = 64
RET_DV = 128
RET_QK = RET_HEADS * RET_DK
RET_V = RET_HEADS * RET_DV
CHUNK = 128
FOX_HEADS = 8
FOX_DH = 64
FOX_W = FOX_HEADS * FOX_DH
D_FF = 2816
ROPE_BASE = 10000.0
EPS = 1e-6

Z_QR, Z_KR, Z_VR, Z_GR = 0, 256, 512, 1024
Z_QF, Z_KF, Z_VF, Z_FF = 1536, 2048, 2560, 3072
Z_ROWS = 3088
AUG = 16
FOX_ROWS = FOX_DH + AUG
NEG = -1e30

TM_IN = 512
TS_RET = 1024
TQ = 512
FOX_HP = 2
LOG2E = math.log2(math.e)
TM_MERGE = 512
TM_FFN = 512
VMEM_LIMIT = 56 * 1024 * 1024

NT_DIMS = (((1,), (1,)), ((), ()))
TN_DIMS = (((0,), (0,)), ((), ()))


def _split3(c):
    hi = c.astype(BF16).astype(F32)
    r = c - hi
    mid = r.astype(BF16).astype(F32)
    lo = (r - mid).astype(BF16).astype(F32)
    return hi, mid, lo


def _inproj_kernel(x_ref, gmix_ref, wt_ref, cos_ref, sin_ref, gq_ref, gk_ref, bf_ref,
                   qr_ref, kr_ref, vr_ref, gr_ref, qf_ref, kf_ref, vf_ref,
                   carry_ref):
    tm = x_ref.shape[1]

    @pl.when(pl.program_id(1) == 0)
    def _():
        carry_ref[...] = jnp.zeros_like(carry_ref)

    x = x_ref[0]
    ms = jnp.mean(x * x, axis=-1, keepdims=True)
    h = (x * lax.rsqrt(ms + EPS) * gmix_ref[...]).astype(BF16)
    z = lax.dot_general(wt_ref[...], h, NT_DIMS, preferred_element_type=F32)

    cos = cos_ref[...]
    sin = sin_ref[...]
    half = RET_DK // 2
    for hh in range(RET_HEADS):
        for base, ref, scl in ((Z_QR, qr_ref, 1.0), (Z_KR, kr_ref, RET_DK ** -0.5)):
            r0 = base + hh * RET_DK
            x1 = z[r0:r0 + half]
            x2 = z[r0 + half:r0 + RET_DK]
            o1 = (x1 * cos - x2 * sin) * scl
            o2 = (x1 * sin + x2 * cos) * scl
            ref[0, hh * RET_DK:hh * RET_DK + half, :] = o1.astype(BF16)
            ref[0, hh * RET_DK + half:(hh + 1) * RET_DK, :] = o2.astype(BF16)

    vr_ref[0] = z[Z_VR:Z_VR + RET_V].astype(BF16)
    g = z[Z_GR:Z_GR + RET_V]
    gr_ref[0] = (g * jax.nn.sigmoid(g)).astype(BF16)

    u = z[Z_FF:Z_FF + FOX_HEADS] + bf_ref[...]
    ls = jnp.minimum(u, 0.0) - jnp.log1p(jnp.exp(-jnp.abs(u)))
    lane = lax.broadcasted_iota(jnp.int32, ls.shape, 1)
    c = ls
    d = 1
    while d < tm:
        c = c + jnp.where(lane >= d, pltpu.roll(c, d, axis=1), 0.0)
        d *= 2
    c = c + carry_ref[:, 0:1]
    carry_ref[...] = jnp.broadcast_to(c[:, tm - 1:tm], carry_ref.shape)

    ridx = lax.broadcasted_iota(jnp.int32, (AUG, tm), 0)
    scale = FOX_DH ** -0.5 * LOG2E
    c2 = c * LOG2E
    for hh in range(FOX_HEADS):
        r0 = hh * FOX_DH
        q = z[Z_QF + r0:Z_QF + r0 + FOX_DH]
        k = z[Z_KF + r0:Z_KF + r0 + FOX_DH]
        qn = q * lax.rsqrt(jnp.mean(q * q, axis=0, keepdims=True) + EPS) * gq_ref[...]
        kn = k * lax.rsqrt(jnp.mean(k * k, axis=0, keepdims=True) + EPS) * gk_ref[...]
        c_hi, c_mid, c_lo = _split3(c2[hh:hh + 1, :])
        q_aug = jnp.where(ridx == 0, c_hi, jnp.where(ridx == 1, c_mid,
                          jnp.where(ridx == 2, c_lo, jnp.where(ridx < 6, 1.0, 0.0))))
        k_aug = jnp.where(ridx == 3, -c_hi, jnp.where(ridx == 4, -c_mid,
                          jnp.where(ridx == 5, -c_lo, jnp.where(ridx < 3, 1.0, 0.0))))
        qf_ref[0, hh, 0:FOX_DH, :] = (qn * scale).astype(BF16)
        qf_ref[0, hh, FOX_DH:FOX_ROWS, :] = q_aug.astype(BF16)
        kf_ref[0, hh, 0:FOX_DH, :] = kn.astype(BF16)
        kf_ref[0, hh, FOX_DH:FOX_ROWS, :] = k_aug.astype(BF16)
        vf_ref[0, hh, 0:FOX_DH, :] = z[Z_VF + r0:Z_VF + r0 + FOX_DH].astype(BF16)
        vf_ref[0, hh, FOX_DH:FOX_ROWS, :] = jnp.where(ridx == 0, 1.0, 0.0).astype(BF16)


def _inproj(x, gmix, wt, cos_t, sin_t, gq, gk, bf):
    B, S, D = x.shape
    tm = min(TM_IN, S)
    grid = (B, S // tm)
    const = lambda b, s: (0, 0)
    fm = lambda rows: pl.BlockSpec((1, rows, tm), lambda b, s: (b, 0, s))
    fox = pl.BlockSpec((1, FOX_HEADS, FOX_ROWS, tm), lambda b, s: (b, 0, 0, s))
    out_shape = (
        jax.ShapeDtypeStruct((B, RET_QK, S), BF16),
        jax.ShapeDtypeStruct((B, RET_QK, S), BF16),
        jax.ShapeDtypeStruct((B, RET_V, S), BF16),
        jax.ShapeDtypeStruct((B, RET_V, S), BF16),
        jax.ShapeDtypeStruct((B, FOX_HEADS, FOX_ROWS, S), BF16),
        jax.ShapeDtypeStruct((B, FOX_HEADS, FOX_ROWS, S), BF16),
        jax.ShapeDtypeStruct((B, FOX_HEADS, FOX_ROWS, S), BF16),
    )
    return pl.pallas_call(
        _inproj_kernel,
        grid=grid,
        in_specs=[
            pl.BlockSpec((1, tm, D), lambda b, s: (b, s, 0)),
            pl.BlockSpec((1, D), const),
            pl.BlockSpec((Z_ROWS, D), const, pipeline_mode=pl.Buffered(1)),
            pl.BlockSpec((RET_DK // 2, tm), lambda b, s: (0, s)),
            pl.BlockSpec((RET_DK // 2, tm), lambda b, s: (0, s)),
            pl.BlockSpec((FOX_DH, 1), const),
            pl.BlockSpec((FOX_DH, 1), const),
            pl.BlockSpec((FOX_HEADS, 1), const),
        ],
        out_specs=(fm(RET_QK), fm(RET_QK), fm(RET_V), fm(RET_V), fox, fox, fox),
        out_shape=out_shape,
        scratch_shapes=[pltpu.VMEM((FOX_HEADS, 128), F32)],
        compiler_params=pltpu.CompilerParams(
            dimension_semantics=("arbitrary", "arbitrary"),
            vmem_limit_bytes=VMEM_LIMIT),
        name="inproj",
    )(x, gmix, wt, cos_t, sin_t, gq, gk, bf)


def _retention_kernel(q_ref, k_ref, v_ref, g_ref, decay_ref, zeta_ref, xi_ref, gch_ref, gn_ref,
                      o_ref, st_ref):
    ts = q_ref.shape[2]

    @pl.when(pl.program_id(1) == 0)
    def _():
        st_ref[...] = jnp.zeros_like(st_ref)

    def chunk_body(ci, _):
        t0 = pl.multiple_of(ci * CHUNK, CHUNK)
        for hh in range(RET_HEADS):
            qT = q_ref[0, hh * RET_DK:(hh + 1) * RET_DK, pl.ds(t0, CHUNK)]
            kT = k_ref[0, hh * RET_DK:(hh + 1) * RET_DK, pl.ds(t0, CHUNK)]
            vT = v_ref[0, hh * RET_DV:(hh + 1) * RET_DV, pl.ds(t0, CHUNK)]
            sg = g_ref[0, hh * RET_DV:(hh + 1) * RET_DV, pl.ds(t0, CHUNK)]
            sT = lax.dot_general(kT, qT, TN_DIMS, preferred_element_type=F32)
            pT = (sT * decay_ref[hh]).astype(BF16)
            intra = jnp.dot(vT, pT, preferred_element_type=F32)
            st = st_ref[hh]
            inter = lax.dot_general(st.astype(BF16), qT, TN_DIMS,
                                    preferred_element_type=F32)
            o = intra + inter * xi_ref[hh]
            vz = (vT.astype(F32) * zeta_ref[hh]).astype(BF16)
            kv = lax.dot_general(kT, vz, NT_DIMS, preferred_element_type=F32)
            st_ref[hh] = gch_ref[hh] * st + kv
            mu = jnp.mean(o, axis=0, keepdims=True)
            oc = o - mu
            var = jnp.mean(oc * oc, axis=0, keepdims=True)
            on = oc * lax.rsqrt(var + EPS) * gn_ref[hh]
            o_ref[0, hh * RET_DV:(hh + 1) * RET_DV, pl.ds(t0, CHUNK)] = (
                sg.astype(F32) * on).astype(BF16)
        return 0

    lax.fori_loop(0, ts // CHUNK, chunk_body, 0)


def _retention(qr, kr, vr, gr, decay_t, zeta, xi, gch, gn):
    B, _, S = qr.shape
    ts = min(TS_RET, S)
    grid = (B, S // ts)
    fm = lambda rows: pl.BlockSpec((1, rows, ts), lambda b, s: (b, 0, s))
    c3 = lambda b, s: (0, 0, 0)
    return pl.pallas_call(
        _retention_kernel,
        grid=grid,
        in_specs=[
            fm(RET_QK), fm(RET_QK), fm(RET_V), fm(RET_V),
            pl.BlockSpec((RET_HEADS, CHUNK, CHUNK), c3),
            pl.BlockSpec((RET_HEADS, 1, CHUNK), c3),
            pl.BlockSpec((RET_HEADS, 1, CHUNK), c3),
            pl.BlockSpec((RET_HEADS, 1, 1), c3),
            pl.BlockSpec((RET_HEADS, RET_DV, 1), c3),
        ],
        out_specs=fm(RET_V),
        out_shape=jax.ShapeDtypeStruct((B, RET_V, S), BF16),
        scratch_shapes=[pltpu.VMEM((RET_HEADS, RET_DK, RET_DV), F32)],
        compiler_params=pltpu.CompilerParams(
            dimension_semantics=("arbitrary", "arbitrary"),
            vmem_limit_bytes=VMEM_LIMIT),
        name="retention",
    )(qr, kr, vr, gr, decay_t, zeta, xi, gch, gn)


def _fox_kernel(q_ref, k_ref, v_ref, o_ref, *, tq, tk):
    hp = q_ref.shape[1]
    S = q_ref.shape[3]
    row = lax.broadcasted_iota(jnp.int32, (tk, tq), 0)
    col = lax.broadcasted_iota(jnp.int32, (tk, tq), 1)
    future = row > col

    def q_body(qi, _):
        q0 = pl.multiple_of(qi * tq, tq)
        qTs = [q_ref[0, hh, :, pl.ds(q0, tq)] for hh in range(hp)]

        def kv_step(j, carry, masked):
            k0 = pl.multiple_of(j * tk, tk)
            out = []
            for hh in range(hp):
                m, acc = carry[hh]
                kT = k_ref[0, hh, :, pl.ds(k0, tk)]
                vT = v_ref[0, hh, :, pl.ds(k0, tk)]
                s = lax.dot_general(kT, qTs[hh], TN_DIMS, preferred_element_type=F32)
                if masked:
                    s = jnp.where(future, NEG, s)
                m_new = jnp.maximum(m, jnp.max(s, axis=0, keepdims=True))
                p = jnp.exp2(s - m_new)
                alpha = jnp.exp2(m - m_new)
                acc = acc * alpha + jnp.dot(vT, p.astype(BF16), preferred_element_type=F32)
                out.append((m_new, acc))
            return tuple(out)

        init = tuple((jnp.full((1, tq), NEG, F32), jnp.zeros((FOX_ROWS, tq), F32))
                     for _ in range(hp))
        carry = lax.fori_loop(0, qi, lambda j, c: kv_step(j, c, False), init)
        carry = kv_step(qi, carry, True)
        for hh in range(hp):
            acc = carry[hh][1]
            o = acc[0:FOX_DH] / acc[FOX_DH:FOX_DH + 1]
            o_ref[0, hh * FOX_DH:(hh + 1) * FOX_DH, pl.ds(q0, tq)] = o.astype(BF16)
        return 0

    lax.fori_loop(0, S // tq, q_body, 0)


def _fox(qf, kf, vf):
    B, H, R, S = qf.shape
    tq = min(TQ, S)
    hp = FOX_HP
    spec = pl.BlockSpec((1, hp, R, S), lambda b, h: (b, h, 0, 0))
    return pl.pallas_call(
        functools.partial(_fox_kernel, tq=tq, tk=tq),
        grid=(B, H // hp),
        in_specs=[spec, spec, spec],
        out_specs=pl.BlockSpec((1, hp * FOX_DH, S), lambda b, h: (b, h, 0)),
        out_shape=jax.ShapeDtypeStruct((B, FOX_W, S), BF16),
        compiler_params=pltpu.CompilerParams(
            dimension_semantics=("arbitrary", "arbitrary"),
            vmem_limit_bytes=VMEM_LIMIT),
        name="fox",
    )(qf, kf, vf)


def _merge_kernel(x_ref, or_ref, of_ref, gmix_ref, wa_ref, wro_ref, wfo_ref, wout_ref, o_ref):
    x = x_ref[0]
    ms = jnp.mean(x * x, axis=-1, keepdims=True)
    h = (x * lax.rsqrt(ms + EPS) * gmix_ref[...]).astype(BF16)
    a = jnp.dot(h, wa_ref[...], preferred_element_type=F32)
    y_r = lax.dot_general(or_ref[0], wro_ref[...], TN_DIMS, preferred_element_type=F32)
    y_f = lax.dot_general(of_ref[0], wfo_ref[...], TN_DIMS, preferred_element_type=F32)
    merged = (jax.nn.sigmoid(a[:, :D_MODEL]) * y_r
              + jax.nn.sigmoid(a[:, D_MODEL:]) * y_f).astype(BF16)
    o_ref[0] = x + jnp.dot(merged, wout_ref[...], preferred_element_type=F32)


def _merge(x, o_r, o_f, gmix, wa, wro, wfo, wout):
    B, S, D = x.shape
    tm = min(TM_MERGE, S)
    const = lambda b, s: (0, 0)
    w = lambda shape: pl.BlockSpec(shape, const, pipeline_mode=pl.Buffered(1))
    return pl.pallas_call(
        _merge_kernel,
        grid=(B, S // tm),
        in_specs=[
            pl.BlockSpec((1, tm, D), lambda b, s: (b, s, 0)),
            pl.BlockSpec((1, RET_V, tm), lambda b, s: (b, 0, s)),
            pl.BlockSpec((1, FOX_W, tm), lambda b, s: (b, 0, s)),
            pl.BlockSpec((1, D), const),
            w((D, 2 * D)), w((RET_V, D)), w((FOX_W, D)), w((D, D)),
        ],
        out_specs=pl.BlockSpec((1, tm, D), lambda b, s: (b, s, 0)),
        out_shape=jax.ShapeDtypeStruct((B, S, D), F32),
        compiler_params=pltpu.CompilerParams(
            dimension_semantics=("arbitrary", "arbitrary"),
            vmem_limit_bytes=VMEM_LIMIT),
        name="merge",
    )(x, o_r, o_f, gmix, wa, wro, wfo, wout)


def _ffn_kernel(x_ref, g_ref, wg_ref, wu_ref, wd_ref, o_ref):
    x = x_ref[0]
    ms = jnp.mean(x * x, axis=-1, keepdims=True)
    h = (x * lax.rsqrt(ms + EPS) * g_ref[...]).astype(BF16)
    gate = jnp.dot(h, wg_ref[...], preferred_element_type=F32)
    up = jnp.dot(h, wu_ref[...], preferred_element_type=F32)
    act = (gate * jax.nn.sigmoid(gate) * up).astype(BF16)
    o_ref[0] = x + jnp.dot(act, wd_ref[...], preferred_element_type=F32)


def _ffn(x, g, wg, wu, wd):
    B, S, D = x.shape
    tm = min(TM_FFN, S)
    const = lambda b, s: (0, 0)
    w = lambda shape: pl.BlockSpec(shape, const, pipeline_mode=pl.Buffered(1))
    return pl.pallas_call(
        _ffn_kernel,
        grid=(B, S // tm),
        in_specs=[
            pl.BlockSpec((1, tm, D), lambda b, s: (b, s, 0)),
            pl.BlockSpec((1, D), const),
            w((D, D_FF)), w((D, D_FF)), w((D_FF, D)),
        ],
        out_specs=pl.BlockSpec((1, tm, D), lambda b, s: (b, s, 0)),
        out_shape=jax.ShapeDtypeStruct((B, S, D), F32),
        compiler_params=pltpu.CompilerParams(
            dimension_semantics=("arbitrary", "arbitrary"),
            vmem_limit_bytes=VMEM_LIMIT),
        name="ffn",
    )(x, g, wg, wu, wd)


def _rope_tables(S):
    half = RET_DK // 2
    pos = jnp.arange(S, dtype=F32)
    inv_freq = 1.0 / (ROPE_BASE ** (jnp.arange(half, dtype=F32) / half))
    ang = pos[:, None] * inv_freq[None, :]
    return jnp.cos(ang).T, jnp.sin(ang).T


def _retention_tables():
    log_g = jnp.log1p(-(2.0 ** (-5.0 - jnp.arange(RET_HEADS, dtype=F32))))
    idx = jnp.arange(CHUNK, dtype=F32)
    diff = idx[:, None] - idx[None, :]
    decay = jnp.where(diff[None] >= 0,
                      jnp.exp(jnp.maximum(diff, 0.0)[None] * log_g[:, None, None]), 0.0)
    decay_t = jnp.swapaxes(decay, 1, 2)
    zeta = jnp.exp((CHUNK - 1.0 - idx)[None, :] * log_g[:, None])[:, None, :]
    xi = jnp.exp((idx + 1.0)[None, :] * log_g[:, None])[:, None, :]
    gch = jnp.exp(CHUNK * log_g)[:, None, None]
    return decay_t, zeta, xi, gch


def kernel(x, g_mix, w_in, b_forget, g_ret_norm, w_ret_o, g_fox_q, g_fox_k, w_fox_o,
           w_out, g_ffn, w_gate, w_up, w_down):
    B, S, D = x.shape
    depth = g_mix.shape[0]
    cos_t, sin_t = _rope_tables(S)
    decay_t, zeta, xi, gch = _retention_tables()
    n_z = Z_FF + FOX_HEADS
    for l in range(depth):
        wt = jnp.pad(w_in[l][:, :n_z].T, ((0, Z_ROWS - n_z), (0, 0))).astype(BF16)
        wa = w_in[l][:, n_z:].astype(BF16)
        gmix = g_mix[l][None, :]
        qr, kr, vr, gr, qf, kf, vf = _inproj(
            x, gmix, wt, cos_t, sin_t,
            g_fox_q[l][:, None], g_fox_k[l][:, None], b_forget[l][:, None])
        o_r = _retention(qr, kr, vr, gr, decay_t, zeta, xi, gch,
                         g_ret_norm[l].reshape(RET_HEADS, RET_DV, 1))
        o_f = _fox(qf, kf, vf)
        x = _merge(x, o_r, o_f, gmix, wa, w_ret_o[l].astype(BF16), w_fox_o[l].astype(BF16),
                   w_out[l].astype(BF16))
        x = _ffn(x, g_ffn[l][None, :], w_gate[l].astype(BF16), w_up[l].astype(BF16),
                 w_down[l].astype(BF16))
    return x
```

```python
import functools
import math

import jax
import jax.numpy as jnp
from jax import lax
from jax.experimental import pallas as pl
from jax.experimental.pallas import tpu as pltpu

F32 = jnp.float32
BF16 = jnp.bfloat16

D_MODEL = 1024
RET_HEADS = 4
RET_DK = 64
RET_DV = 128
RET_QK = RET_HEADS * RET_DK
RET_V = RET_HEADS * RET_DV
CHUNK = 128
FOX_HEADS = 8
FOX_DH = 64
FOX_W = FOX_HEADS * FOX_DH
D_FF = 2816
ROPE_BASE = 10000.0
EPS = 1e-6

Z_QR, Z_KR, Z_VR, Z_GR = 0, 256, 512, 1024
Z_QF, Z_KF, Z_VF, Z_FF = 1536, 2048, 2560, 3072
Z_ROWS = 3088
AUG = 16
FOX_ROWS = FOX_DH + AUG
NEG = -1e30

TM_IN = 512
TS_RET = 1024
TQ = 1024
FOX_HP = 1
LOG2E = math.log2(math.e)
TM_MERGE = 512
TM_FFN = 512
VMEM_LIMIT = 56 * 1024 * 1024

NT_DIMS = (((1,), (1,)), ((), ()))
TN_DIMS = (((0,), (0,)), ((), ()))


def _split3(c):
    hi = c.astype(BF16).astype(F32)
    r = c - hi
    mid = r.astype(BF16).astype(F32)
    lo = (r - mid).astype(BF16).astype(F32)
    return hi, mid, lo


def _inproj_kernel(x_ref, gmix_ref, wt_ref, cos_ref, sin_ref, gq_ref, gk_ref, bf_ref,
                   qr_ref, kr_ref, vr_ref, gr_ref, qf_ref, kf_ref, vf_ref,
                   carry_ref):
    tm = x_ref.shape[1]

    @pl.when(pl.program_id(1) == 0)
    def _():
        carry_ref[...] = jnp.zeros_like(carry_ref)

    x = x_ref[0]
    ms = jnp.mean(x * x, axis=-1, keepdims=True)
    h = (x * lax.rsqrt(ms + EPS) * gmix_ref[...]).astype(BF16)
    z = lax.dot_general(wt_ref[...], h, NT_DIMS, preferred_element_type=F32)

    cos = cos_ref[...]
    sin = sin_ref[...]
    half = RET_DK // 2
    for hh in range(RET_HEADS):
        for base, ref, scl in ((Z_QR, qr_ref, 1.0), (Z_KR, kr_ref, RET_DK ** -0.5)):
            r0 = base + hh * RET_DK
            x1 = z[r0:r0 + half]
            x2 = z[r0 + half:r0 + RET_DK]
            o1 = (x1 * cos - x2 * sin) * scl
            o2 = (x1 * sin + x2 * cos) * scl
            ref[0, hh * RET_DK:hh * RET_DK + half, :] = o1.astype(BF16)
            ref[0, hh * RET_DK + half:(hh + 1) * RET_DK, :] = o2.astype(BF16)

    vr_ref[0] = z[Z_VR:Z_VR + RET_V].astype(BF16)
    g = z[Z_GR:Z_GR + RET_V]
    gr_ref[0] = (g * jax.nn.sigmoid(g)).astype(BF16)

    u = z[Z_FF:Z_FF + FOX_HEADS] + bf_ref[...]
    ls = jnp.minimum(u, 0.0) - jnp.log1p(jnp.exp(-jnp.abs(u)))
    lane = lax.broadcasted_iota(jnp.int32, ls.shape, 1)
    c = ls
    d = 1
    while d < tm:
        c = c + jnp.where(lane >= d, pltpu.roll(c, d, axis=1), 0.0)
        d *= 2
    c = c + carry_ref[:, 0:1]
    carry_ref[...] = jnp.broadcast_to(c[:, tm - 1:tm], carry_ref.shape)

    ridx = lax.broadcasted_iota(jnp.int32, (AUG, tm), 0)
    scale = FOX_DH ** -0.5 * LOG2E
    c2 = c * LOG2E
    for hh in range(FOX_HEADS):
        r0 = hh * FOX_DH
        q = z[Z_QF + r0:Z_QF + r0 + FOX_DH]
        k = z[Z_KF + r0:Z_KF + r0 + FOX_DH]
        qn = q * lax.rsqrt(jnp.mean(q * q, axis=0, keepdims=True) + EPS) * gq_ref[...]
        kn = k * lax.rsqrt(jnp.mean(k * k, axis=0, keepdims=True) + EPS) * gk_ref[...]
        c_hi, c_mid, c_lo = _split3(c2[hh:hh + 1, :])
        q_aug = jnp.where(ridx == 0, c_hi, jnp.where(ridx == 1, c_mid,
                          jnp.where(ridx == 2, c_lo, jnp.where(ridx < 6, 1.0, 0.0))))
        k_aug = jnp.where(ridx == 3, -c_hi, jnp.where(ridx == 4, -c_mid,
                          jnp.where(ridx == 5, -c_lo, jnp.where(ridx < 3, 1.0, 0.0))))
        qf_ref[0, hh, 0:FOX_DH, :] = (qn * scale).astype(BF16)
        qf_ref[0, hh, FOX_DH:FOX_ROWS, :] = q_aug.astype(BF16)
        kf_ref[0, hh, 0:FOX_DH, :] = kn.astype(BF16)
        kf_ref[0, hh, FOX_DH:FOX_ROWS, :] = k_aug.astype(BF16)
        vf_ref[0, hh, 0:FOX_DH, :] = z[Z_VF + r0:Z_VF + r0 + FOX_DH].astype(BF16)
        vf_ref[0, hh, FOX_DH:FOX_ROWS, :] = jnp.where(ridx == 0, 1.0, 0.0).astype(BF16)


def _inproj(x, gmix, wt, cos_t, sin_t, gq, gk, bf):
    B, S, D = x.shape
    tm = min(TM_IN, S)
    grid = (B, S // tm)
    const = lambda b, s: (0, 0)
    fm = lambda rows: pl.BlockSpec((1, rows, tm), lambda b, s: (b, 0, s))
    fox = pl.BlockSpec((1, FOX_HEADS, FOX_ROWS, tm), lambda b, s: (b, 0, 0, s))
    out_shape = (
        jax.ShapeDtypeStruct((B, RET_QK, S), BF16),
        jax.ShapeDtypeStruct((B, RET_QK, S), BF16),
        jax.ShapeDtypeStruct((B, RET_V, S), BF16),
        jax.ShapeDtypeStruct((B, RET_V, S), BF16),
        jax.ShapeDtypeStruct((B, FOX_HEADS, FOX_ROWS, S), BF16),
        jax.ShapeDtypeStruct((B, FOX_HEADS, FOX_ROWS, S), BF16),
        jax.ShapeDtypeStruct((B, FOX_HEADS, FOX_ROWS, S), BF16),
    )
    return pl.pallas_call(
        _inproj_kernel,
        grid=grid,
        in_specs=[
            pl.BlockSpec((1, tm, D), lambda b, s: (b, s, 0)),
            pl.BlockSpec((1, D), const),
            pl.BlockSpec((Z_ROWS, D), const, pipeline_mode=pl.Buffered(1)),
            pl.BlockSpec((RET_DK // 2, tm), lambda b, s: (0, s)),
            pl.BlockSpec((RET_DK // 2, tm), lambda b, s: (0, s)),
            pl.BlockSpec((FOX_DH, 1), const),
            pl.BlockSpec((FOX_DH, 1), const),
            pl.BlockSpec((FOX_HEADS, 1), const),
        ],
        out_specs=(fm(RET_QK), fm(RET_QK), fm(RET_V), fm(RET_V), fox, fox, fox),
        out_shape=out_shape,
        scratch_shapes=[pltpu.VMEM((FOX_HEADS, 128), F32)],
        compiler_params=pltpu.CompilerParams(
            dimension_semantics=("arbitrary", "arbitrary"),
            vmem_limit_bytes=VMEM_LIMIT),
        name="inproj",
    )(x, gmix, wt, cos_t, sin_t, gq, gk, bf)


def _retention_kernel(q_ref, k_ref, v_ref, g_ref, decay_ref, zeta_ref, xi_ref, gch_ref, gn_ref,
                      o_ref, st_ref):
    ts = q_ref.shape[2]

    @pl.when(pl.program_id(1) == 0)
    def _():
        st_ref[...] = jnp.zeros_like(st_ref)

    def chunk_body(ci, _):
        t0 = pl.multiple_of(ci * CHUNK, CHUNK)
        for hh in range(RET_HEADS):
            qT = q_ref[0, hh * RET_DK:(hh + 1) * RET_DK, pl.ds(t0, CHUNK)]
            kT = k_ref[0, hh * RET_DK:(hh + 1) * RET_DK, pl.ds(t0, CHUNK)]
            vT = v_ref[0, hh * RET_DV:(hh + 1) * RET_DV, pl.ds(t0, CHUNK)]
            sg = g_ref[0, hh * RET_DV:(hh + 1) * RET_DV, pl.ds(t0, CHUNK)]
            sT = lax.dot_general(kT, qT, TN_DIMS, preferred_element_type=F32)
            pT = (sT * decay_ref[hh]).astype(BF16)
            intra = jnp.dot(vT, pT, preferred_element_type=F32)
            st = st_ref[hh]
            inter = lax.dot_general(st.astype(BF16), qT, TN_DIMS,
                                    preferred_element_type=F32)
            o = intra + inter * xi_ref[hh]
            vz = (vT.astype(F32) * zeta_ref[hh]).astype(BF16)
            kv = lax.dot_general(kT, vz, NT_DIMS, preferred_element_type=F32)
            st_ref[hh] = gch_ref[hh] * st + kv
            mu = jnp.mean(o, axis=0, keepdims=True)
            oc = o - mu
            var = jnp.mean(oc * oc, axis=0, keepdims=True)
            on = oc * lax.rsqrt(var + EPS) * gn_ref[hh]
            o_ref[0, hh * RET_DV:(hh + 1) * RET_DV, pl.ds(t0, CHUNK)] = (
                sg.astype(F32) * on).astype(BF16)
        return 0

    lax.fori_loop(0, ts // CHUNK, chunk_body, 0)


def _retention(qr, kr, vr, gr, decay_t, zeta, xi, gch, gn):
    B, _, S = qr.shape
    ts = min(TS_RET, S)
    grid = (B, S // ts)
    fm = lambda rows: pl.BlockSpec((1, rows, ts), lambda b, s: (b, 0, s))
    c3 = lambda b, s: (0, 0, 0)
    return pl.pallas_call(
        _retention_kernel,
        grid=grid,
        in_specs=[
            fm(RET_QK), fm(RET_QK), fm(RET_V), fm(RET_V),
            pl.BlockSpec((RET_HEADS, CHUNK, CHUNK), c3),
            pl.BlockSpec((RET_HEADS, 1, CHUNK), c3),
            pl.BlockSpec((RET_HEADS, 1, CHUNK), c3),
            pl.BlockSpec((RET_HEADS, 1, 1), c3),
            pl.BlockSpec((RET_HEADS, RET_DV, 1), c3),
        ],
        out_specs=fm(RET_V),
        out_shape=jax.ShapeDtypeStruct((B, RET_V, S), BF16),
        scratch_shapes=[pltpu.VMEM((RET_HEADS, RET_DK, RET_DV), F32)],
        compiler_params=pltpu.CompilerParams(
            dimension_semantics=("arbitrary", "arbitrary"),
            vmem_limit_bytes=VMEM_LIMIT),
        name="retention",
    )(qr, kr, vr, gr, decay_t, zeta, xi, gch, gn)


def _fox_kernel(q_ref, k_ref, v_ref, o_ref, s_scr, p_scr, acc_scr, m_scr, al_scr, mx_scr, *, tq):
    hp = q_ref.shape[1]
    S = q_ref.shape[3]
    tk = tq

    def q_body(qi, _):
        q0 = pl.multiple_of(qi * tq, tq)
        qTs = [q_ref[0, hh, :, pl.ds(q0, tq)] for hh in range(hp)]

        def scores(hh, tile, masked):
            k0 = pl.multiple_of(tile * tk, tk)
            kT = k_ref[0, hh, :, pl.ds(k0, tk)]
            s = lax.dot_general(kT, qTs[hh], TN_DIMS, preferred_element_type=F32)
            if masked:
                row = lax.broadcasted_iota(jnp.int32, (tk, tq), 0)
                col = lax.broadcasted_iota(jnp.int32, (tk, tq), 1)
                s = jnp.where(row > col, NEG, s)
            s_scr[hh] = s
            mx_scr[hh] = jnp.broadcast_to(jnp.max(s, axis=0, keepdims=True), (8, tq))

        def pv(hh, tile):
            k0 = pl.multiple_of(tile * tk, tk)
            vT = v_ref[0, hh, :, pl.ds(k0, tk)]
            acc_scr[hh] = (acc_scr[hh] * al_scr[hh][0:1]
                           + jnp.dot(vT, p_scr[hh], preferred_element_type=F32))

        def softmax(hh):
            m = m_scr[hh]
            m_new = jnp.maximum(m, mx_scr[hh])
            p_scr[hh] = jnp.exp2(s_scr[hh] - m_new[0:1]).astype(BF16)
            al_scr[hh] = jnp.exp2(m - m_new)
            m_scr[hh] = m_new

        for hh in range(hp):
            p_scr[hh] = jnp.zeros((tk, tq), BF16)
            acc_scr[hh] = jnp.zeros((FOX_ROWS, tq), F32)
            m_scr[hh] = jnp.full((8, tq), NEG, F32)
            al_scr[hh] = jnp.ones((8, tq), F32)
            scores(hh, qi, True)

        def step(u, _):
            prev_tile = jnp.where(u <= 1, qi, u - 2)
            for hh in range(hp):
                pv(hh, prev_tile)
            for hh in range(hp):
                softmax(hh)
            for hh in range(hp):
                scores(hh, u, False)
            return 0

        lax.fori_loop(0, qi, step, 0)

        prev_tile = jnp.where(qi <= 1, qi, qi - 2)
        last_tile = jnp.maximum(qi - 1, 0)
        for hh in range(hp):
            pv(hh, prev_tile)
            softmax(hh)
            pv(hh, last_tile)
            acc = acc_scr[hh]
            o = acc[0:FOX_DH] / acc[FOX_DH:FOX_DH + 1]
            o_ref[0, hh * FOX_DH:(hh + 1) * FOX_DH, pl.ds(q0, tq)] = o.astype(BF16)
        return 0

    lax.fori_loop(0, S // tq, q_body, 0)


def _fox(qf, kf, vf):
    B, H, R, S = qf.shape
    tq = min(TQ, S)
    hp = FOX_HP
    spec = pl.BlockSpec((1, hp, R, S), lambda b, h: (b, h, 0, 0))
    return pl.pallas_call(
        functools.partial(_fox_kernel, tq=tq),
        grid=(B, H // hp),
        in_specs=[spec, spec, spec],
        out_specs=pl.BlockSpec((1, hp * FOX_DH, S), lambda b, h: (b, h, 0)),
        out_shape=jax.ShapeDtypeStruct((B, FOX_W, S), BF16),
        scratch_shapes=[
            pltpu.VMEM((hp, tq, tq), F32),
            pltpu.VMEM((hp, tq, tq), BF16),
            pltpu.VMEM((hp, R, tq), F32),
            pltpu.VMEM((hp, 8, tq), F32),
            pltpu.VMEM((hp, 8, tq), F32),
            pltpu.VMEM((hp, 8, tq), F32),
        ],
        compiler_params=pltpu.CompilerParams(
            dimension_semantics=("arbitrary", "arbitrary"),
            vmem_limit_bytes=VMEM_LIMIT),
        name="fox",
    )(qf, kf, vf)


def _merge_kernel(x_ref, or_ref, of_ref, gmix_ref, wa_ref, wro_ref, wfo_ref, wout_ref, o_ref):
    x = x_ref[0]
    ms = jnp.mean(x * x, axis=-1, keepdims=True)
    h = (x * lax.rsqrt(ms + EPS) * gmix_ref[...]).astype(BF16)
    a = jnp.dot(h, wa_ref[...], preferred_element_type=F32)
    y_r = lax.dot_general(or_ref[0], wro_ref[...], TN_DIMS, preferred_element_type=F32)
    y_f = lax.dot_general(of_ref[0], wfo_ref[...], TN_DIMS, preferred_element_type=F32)
    merged = (jax.nn.sigmoid(a[:, :D_MODEL]) * y_r
              + jax.nn.sigmoid(a[:, D_MODEL:]) * y_f).astype(BF16)
    o_ref[0] = x + jnp.dot(merged, wout_ref[...], preferred_element_type=F32)


def _merge(x, o_r, o_f, gmix, wa, wro, wfo, wout):
    B, S, D = x.shape
    tm = min(TM_MERGE, S)
    const = lambda b, s: (0, 0)
    w = lambda shape: pl.BlockSpec(shape, const, pipeline_mode=pl.Buffered(1))
    return pl.pallas_call(
        _merge_kernel,
        grid=(B, S // tm),
        in_specs=[
            pl.BlockSpec((1, tm, D), lambda b, s: (b, s, 0)),
            pl.BlockSpec((1, RET_V, tm), lambda b, s: (b, 0, s)),
            pl.BlockSpec((1, FOX_W, tm), lambda b, s: (b, 0, s)),
            pl.BlockSpec((1, D), const),
            w((D, 2 * D)), w((RET_V, D)), w((FOX_W, D)), w((D, D)),
        ],
        out_specs=pl.BlockSpec((1, tm, D), lambda b, s: (b, s, 0)),
        out_shape=jax.ShapeDtypeStruct((B, S, D), F32),
        compiler_params=pltpu.CompilerParams(
            dimension_semantics=("arbitrary", "arbitrary"),
            vmem_limit_bytes=VMEM_LIMIT),
        name="merge",
    )(x, o_r, o_f, gmix, wa, wro, wfo, wout)


def _ffn_kernel(x_ref, g_ref, wg_ref, wu_ref, wd_ref, o_ref):
    x = x_ref[0]
    ms = jnp.mean(x * x, axis=-1, keepdims=True)
    h = (x * lax.rsqrt(ms + EPS) * g_ref[...]).astype(BF16)
    gate = jnp.dot(h, wg_ref[...], preferred_element_type=F32)
    up = jnp.dot(h, wu_ref[...], preferred_element_type=F32)
    act = (gate * jax.nn.sigmoid(gate) * up).astype(BF16)
    o_ref[0] = x + jnp.dot(act, wd_ref[...], preferred_element_type=F32)


def _ffn(x, g, wg, wu, wd):
    B, S, D = x.shape
    tm = min(TM_FFN, S)
    const = lambda b, s: (0, 0)
    w = lambda shape: pl.BlockSpec(shape, const, pipeline_mode=pl.Buffered(1))
    return pl.pallas_call(
        _ffn_kernel,
        grid=(B, S // tm),
        in_specs=[
            pl.BlockSpec((1, tm, D), lambda b, s: (b, s, 0)),
            pl.BlockSpec((1, D), const),
            w((D, D_FF)), w((D, D_FF)), w((D_FF, D)),
        ],
        out_specs=pl.BlockSpec((1, tm, D), lambda b, s: (b, s, 0)),
        out_shape=jax.ShapeDtypeStruct((B, S, D), F32),
        compiler_params=pltpu.CompilerParams(
            dimension_semantics=("arbitrary", "arbitrary"),
            vmem_limit_bytes=VMEM_LIMIT),
        name="ffn",
    )(x, g, wg, wu, wd)


def _rope_tables(S):
    half = RET_DK // 2
    pos = jnp.arange(S, dtype=F32)
    inv_freq = 1.0 / (ROPE_BASE ** (jnp.arange(half, dtype=F32) / half))
    ang = pos[:, None] * inv_freq[None, :]
    return jnp.cos(ang).T, jnp.sin(ang).T


def _retention_tables():
    log_g = jnp.log1p(-(2.0 ** (-5.0 - jnp.arange(RET_HEADS, dtype=F32))))
    idx = jnp.arange(CHUNK, dtype=F32)
    diff = idx[:, None] - idx[None, :]
    decay = jnp.where(diff[None] >= 0,
                      jnp.exp(jnp.maximum(diff, 0.0)[None] * log_g[:, None, None]), 0.0)
    decay_t = jnp.swapaxes(decay, 1, 2)
    zeta = jnp.exp((CHUNK - 1.0 - idx)[None, :] * log_g[:, None])[:, None, :]
    xi = jnp.exp((idx + 1.0)[None, :] * log_g[:, None])[:, None, :]
    gch = jnp.exp(CHUNK * log_g)[:, None, None]
    return decay_t, zeta, xi, gch


def kernel(x, g_mix, w_in, b_forget, g_ret_norm, w_ret_o, g_fox_q, g_fox_k, w_fox_o,
           w_out, g_ffn, w_gate, w_up, w_down):
    B, S, D = x.shape
    depth = g_mix.shape[0]
    cos_t, sin_t = _rope_tables(S)
    decay_t, zeta, xi, gch = _retention_tables()
    n_z = Z_FF + FOX_HEADS
    for l in range(depth):
        wt = jnp.pad(w_in[l][:, :n_z].T, ((0, Z_ROWS - n_z), (0, 0))).astype(BF16)
        wa = w_in[l][:, n_z:].astype(BF16)
        gmix = g_mix[l][None, :]
        qr, kr, vr, gr, qf, kf, vf = _inproj(
            x, gmix, wt, cos_t, sin_t,
            g_fox_q[l][:, None], g_fox_k[l][:, None], b_forget[l][:, None])
        o_r = _retention(qr, kr, vr, gr, decay_t, zeta, xi, gch,
                         g_ret_norm[l].reshape(RET_HEADS, RET_DV, 1))
        o_f = _fox(qf, kf, vf)
        x = _merge(x, o_r, o_f, gmix, wa, w_ret_o[l].astype(BF16), w_fox_o[l].astype(BF16),
                   w_out[l].astype(BF16))
        x = _ffn(x, g_ffn[l][None, :], w_gate[l].astype(BF16), w_up[l].astype(BF16),
                 w_down[l].astype(BF16))
    return x
```

```python
import functools
import math

import jax
import jax.numpy as jnp
from jax import lax
from jax.experimental import pallas as pl
from jax.experimental.pallas import tpu as pltpu

F32 = jnp.float32
BF16 = jnp.bfloat16

D_MODEL = 1024
RET_HEADS = 4
RET_DK = 64
RET_DV = 128
RET_QK = RET_HEADS * RET_DK
RET_V = RET_HEADS * RET_DV
CHUNK = 128
FOX_HEADS = 8
FOX_DH = 64
FOX_W = FOX_HEADS * FOX_DH
D_FF = 2816
ROPE_BASE = 10000.0
EPS = 1e-6

Z_QR, Z_KR, Z_VR, Z_GR = 0, 256, 512, 1024
Z_QF, Z_KF, Z_VF, Z_FF = 1536, 2048, 2560, 3072
Z_ROWS = 3088
AUG = 16
FOX_ROWS = FOX_DH + AUG
NEG = -1e30

TM_IN = 512
TS_RET = 1024
RET_UNROLL = 8
TQ = 1024
FOX_HP = 1
LOG2E = math.log2(math.e)
TM_MERGE = 512
TM_FFN = 512
VMEM_LIMIT = 56 * 1024 * 1024

NT_DIMS = (((1,), (1,)), ((), ()))
TN_DIMS = (((0,), (0,)), ((), ()))


def _split3(c):
    hi = c.astype(BF16).astype(F32)
    r = c - hi
    mid = r.astype(BF16).astype(F32)
    lo = (r - mid).astype(BF16).astype(F32)
    return hi, mid, lo


def _inproj_kernel(x_ref, gmix_ref, wt_ref, cos_ref, sin_ref, gq_ref, gk_ref, bf_ref,
                   qr_ref, kr_ref, vr_ref, gr_ref, qf_ref, kf_ref, vf_ref,
                   carry_ref):
    tm = x_ref.shape[1]

    @pl.when(pl.program_id(1) == 0)
    def _():
        carry_ref[...] = jnp.zeros_like(carry_ref)

    x = x_ref[0]
    ms = jnp.mean(x * x, axis=-1, keepdims=True)
    h = (x * lax.rsqrt(ms + EPS) * gmix_ref[...]).astype(BF16)
    z = lax.dot_general(wt_ref[...], h, NT_DIMS, preferred_element_type=F32)

    cos = cos_ref[...]
    sin = sin_ref[...]
    half = RET_DK // 2
    for hh in range(RET_HEADS):
        for base, ref, scl in ((Z_QR, qr_ref, 1.0), (Z_KR, kr_ref, RET_DK ** -0.5)):
            r0 = base + hh * RET_DK
            x1 = z[r0:r0 + half]
            x2 = z[r0 + half:r0 + RET_DK]
            o1 = (x1 * cos - x2 * sin) * scl
            o2 = (x1 * sin + x2 * cos) * scl
            ref[0, hh * RET_DK:hh * RET_DK + half, :] = o1.astype(BF16)
            ref[0, hh * RET_DK + half:(hh + 1) * RET_DK, :] = o2.astype(BF16)

    vr_ref[0] = z[Z_VR:Z_VR + RET_V].astype(BF16)
    g = z[Z_GR:Z_GR + RET_V]
    gr_ref[0] = (g * jax.nn.sigmoid(g)).astype(BF16)

    u = z[Z_FF:Z_FF + FOX_HEADS] + bf_ref[...]
    ls = jnp.minimum(u, 0.0) - jnp.log1p(jnp.exp(-jnp.abs(u)))
    lane = lax.broadcasted_iota(jnp.int32, ls.shape, 1)
    c = ls
    d = 1
    while d < tm:
        c = c + jnp.where(lane >= d, pltpu.roll(c, d, axis=1), 0.0)
        d *= 2
    c = c + carry_ref[:, 0:1]
    carry_ref[...] = jnp.broadcast_to(c[:, tm - 1:tm], carry_ref.shape)

    ridx = lax.broadcasted_iota(jnp.int32, (AUG, tm), 0)
    scale = FOX_DH ** -0.5 * LOG2E
    c2 = c * LOG2E
    for hh in range(FOX_HEADS):
        r0 = hh * FOX_DH
        q = z[Z_QF + r0:Z_QF + r0 + FOX_DH]
        k = z[Z_KF + r0:Z_KF + r0 + FOX_DH]
        qn = q * lax.rsqrt(jnp.mean(q * q, axis=0, keepdims=True) + EPS) * gq_ref[...]
        kn = k * lax.rsqrt(jnp.mean(k * k, axis=0, keepdims=True) + EPS) * gk_ref[...]
        c_hi, c_mid, c_lo = _split3(c2[hh:hh + 1, :])
        q_aug = jnp.where(ridx == 0, c_hi, jnp.where(ridx == 1, c_mid,
                          jnp.where(ridx == 2, c_lo, jnp.where(ridx < 6, 1.0, 0.0))))
        k_aug = jnp.where(ridx == 3, -c_hi, jnp.where(ridx == 4, -c_mid,
                          jnp.where(ridx == 5, -c_lo, jnp.where(ridx < 3, 1.0, 0.0))))
        qf_ref[0, hh, 0:FOX_DH, :] = (qn * scale).astype(BF16)
        qf_ref[0, hh, FOX_DH:FOX_ROWS, :] = q_aug.astype(BF16)
        kf_ref[0, hh, 0:FOX_DH, :] = kn.astype(BF16)
        kf_ref[0, hh, FOX_DH:FOX_ROWS, :] = k_aug.astype(BF16)
        vf_ref[0, hh, 0:FOX_DH, :] = z[Z_VF + r0:Z_VF + r0 + FOX_DH].astype(BF16)
        vf_ref[0, hh, FOX_DH:FOX_ROWS, :] = jnp.where(ridx == 0, 1.0, 0.0).astype(BF16)


def _inproj(x, gmix, wt, cos_t, sin_t, gq, gk, bf):
    B, S, D = x.shape
    tm = min(TM_IN, S)
    grid = (B, S // tm)
    const = lambda b, s: (0, 0)
    fm = lambda rows: pl.BlockSpec((1, rows, tm), lambda b, s: (b, 0, s))
    fox = pl.BlockSpec((1, FOX_HEADS, FOX_ROWS, tm), lambda b, s: (b, 0, 0, s))
    out_shape = (
        jax.ShapeDtypeStruct((B, RET_QK, S), BF16),
        jax.ShapeDtypeStruct((B, RET_QK, S), BF16),
        jax.ShapeDtypeStruct((B, RET_V, S), BF16),
        jax.ShapeDtypeStruct((B, RET_V, S), BF16),
        jax.ShapeDtypeStruct((B, FOX_HEADS, FOX_ROWS, S), BF16),
        jax.ShapeDtypeStruct((B, FOX_HEADS, FOX_ROWS, S), BF16),
        jax.ShapeDtypeStruct((B, FOX_HEADS, FOX_ROWS, S), BF16),
    )
    return pl.pallas_call(
        _inproj_kernel,
        grid=grid,
        in_specs=[
            pl.BlockSpec((1, tm, D), lambda b, s: (b, s, 0)),
            pl.BlockSpec((1, D), const),
            pl.BlockSpec((Z_ROWS, D), const, pipeline_mode=pl.Buffered(1)),
            pl.BlockSpec((RET_DK // 2, tm), lambda b, s: (0, s)),
            pl.BlockSpec((RET_DK // 2, tm), lambda b, s: (0, s)),
            pl.BlockSpec((FOX_DH, 1), const),
            pl.BlockSpec((FOX_DH, 1), const),
            pl.BlockSpec((FOX_HEADS, 1), const),
        ],
        out_specs=(fm(RET_QK), fm(RET_QK), fm(RET_V), fm(RET_V), fox, fox, fox),
        out_shape=out_shape,
        scratch_shapes=[pltpu.VMEM((FOX_HEADS, 128), F32)],
        compiler_params=pltpu.CompilerParams(
            dimension_semantics=("arbitrary", "arbitrary"),
            vmem_limit_bytes=VMEM_LIMIT),
        name="inproj",
    )(x, gmix, wt, cos_t, sin_t, gq, gk, bf)


def _retention_kernel(q_ref, k_ref, v_ref, g_ref, decay_ref, zeta_ref, xi_ref, gch_ref, gn_ref,
                      o_ref, st_ref):
    ts = q_ref.shape[2]

    @pl.when(pl.program_id(1) == 0)
    def _():
        st_ref[...] = jnp.zeros_like(st_ref)

    def chunk_body(ci, _):
        t0 = pl.multiple_of(ci * CHUNK, CHUNK)
        heads = range(RET_HEADS)
        qT = [q_ref[0, hh * RET_DK:(hh + 1) * RET_DK, pl.ds(t0, CHUNK)] for hh in heads]
        kT = [k_ref[0, hh * RET_DK:(hh + 1) * RET_DK, pl.ds(t0, CHUNK)] for hh in heads]
        vT = [v_ref[0, hh * RET_DV:(hh + 1) * RET_DV, pl.ds(t0, CHUNK)] for hh in heads]
        st = [st_ref[hh] for hh in heads]
        sT = [lax.dot_general(kT[hh], qT[hh], TN_DIMS, preferred_element_type=F32) for hh in heads]
        inter = [lax.dot_general(st[hh].astype(BF16), qT[hh], TN_DIMS,
                                 preferred_element_type=F32) for hh in heads]
        for hh in heads:
            vz = (vT[hh].astype(F32) * zeta_ref[hh]).astype(BF16)
            kv = lax.dot_general(kT[hh], vz, NT_DIMS, preferred_element_type=F32)
            st_ref[hh] = gch_ref[hh] * st[hh] + kv
        for hh in heads:
            pT = (sT[hh] * decay_ref[hh]).astype(BF16)
            intra = jnp.dot(vT[hh], pT, preferred_element_type=F32)
            o = intra + inter[hh] * xi_ref[hh]
            mu = jnp.mean(o, axis=0, keepdims=True)
            oc = o - mu
            var = jnp.mean(oc * oc, axis=0, keepdims=True)
            on = oc * lax.rsqrt(var + EPS) * gn_ref[hh]
            sg = g_ref[0, hh * RET_DV:(hh + 1) * RET_DV, pl.ds(t0, CHUNK)]
            o_ref[0, hh * RET_DV:(hh + 1) * RET_DV, pl.ds(t0, CHUNK)] = (
                sg.astype(F32) * on).astype(BF16)
        return 0

    lax.fori_loop(0, ts // CHUNK, chunk_body, 0, unroll=RET_UNROLL)


def _retention(qr, kr, vr, gr, decay_t, zeta, xi, gch, gn):
    B, _, S = qr.shape
    ts = min(TS_RET, S)
    grid = (B, S // ts)
    fm = lambda rows: pl.BlockSpec((1, rows, ts), lambda b, s: (b, 0, s))
    c3 = lambda b, s: (0, 0, 0)
    return pl.pallas_call(
        _retention_kernel,
        grid=grid,
        in_specs=[
            fm(RET_QK), fm(RET_QK), fm(RET_V), fm(RET_V),
            pl.BlockSpec((RET_HEADS, CHUNK, CHUNK), c3),
            pl.BlockSpec((RET_HEADS, 1, CHUNK), c3),
            pl.BlockSpec((RET_HEADS, 1, CHUNK), c3),
            pl.BlockSpec((RET_HEADS, 1, 1), c3),
            pl.BlockSpec((RET_HEADS, RET_DV, 1), c3),
        ],
        out_specs=fm(RET_V),
        out_shape=jax.ShapeDtypeStruct((B, RET_V, S), BF16),
        scratch_shapes=[pltpu.VMEM((RET_HEADS, RET_DK, RET_DV), F32)],
        compiler_params=pltpu.CompilerParams(
            dimension_semantics=("arbitrary", "arbitrary"),
            vmem_limit_bytes=VMEM_LIMIT),
        name="retention",
    )(qr, kr, vr, gr, decay_t, zeta, xi, gch, gn)


def _fox_kernel(q_ref, k_ref, v_ref, o_ref, s_scr, p_scr, acc_scr, m_scr, al_scr, mx_scr, *, tq):
    hp = q_ref.shape[1]
    S = q_ref.shape[3]
    tk = tq
    nq = S // tq

    def load_q(qi):
        q0 = pl.multiple_of(qi * tq, tq)
        return [q_ref[0, hh, :, pl.ds(q0, tq)] for hh in range(hp)]

    def scores(qTs, hh, tile, masked):
        k0 = pl.multiple_of(tile * tk, tk)
        kT = k_ref[0, hh, :, pl.ds(k0, tk)]
        s = lax.dot_general(kT, qTs[hh], TN_DIMS, preferred_element_type=F32)
        if masked:
            row = lax.broadcasted_iota(jnp.int32, (tk, tq), 0)
            col = lax.broadcasted_iota(jnp.int32, (tk, tq), 1)
            s = jnp.where(row > col, NEG, s)
        s_scr[hh] = s
        mx_scr[hh] = jnp.broadcast_to(jnp.max(s, axis=0, keepdims=True), (8, tq))

    def pv(hh, tile):
        k0 = pl.multiple_of(tile * tk, tk)
        vT = v_ref[0, hh, :, pl.ds(k0, tk)]
        acc_scr[hh] = (acc_scr[hh] * al_scr[hh][0:1]
                       + jnp.dot(vT, p_scr[hh], preferred_element_type=F32))

    def softmax(hh):
        m = m_scr[hh]
        m_new = jnp.maximum(m, mx_scr[hh])
        p_scr[hh] = jnp.exp2(s_scr[hh] - m_new[0:1]).astype(BF16)
        al_scr[hh] = jnp.exp2(m - m_new)
        m_scr[hh] = m_new

    def reset(hh):
        p_scr[hh] = jnp.zeros((tk, tq), BF16)
        acc_scr[hh] = jnp.zeros((FOX_ROWS, tq), F32)
        m_scr[hh] = jnp.full((8, tq), NEG, F32)
        al_scr[hh] = jnp.ones((8, tq), F32)

    def finalize(hh, qi):
        q0 = pl.multiple_of(qi * tq, tq)
        acc = acc_scr[hh]
        o = acc[0:FOX_DH] / acc[FOX_DH:FOX_DH + 1]
        o_ref[0, hh * FOX_DH:(hh + 1) * FOX_DH, pl.ds(q0, tq)] = o.astype(BF16)

    def finish(hh, qi, between=None):
        pv(hh, jnp.where(qi <= 1, qi, qi - 2))
        softmax(hh)
        if between is not None:
            between()
        pv(hh, jnp.maximum(qi - 1, 0))
        finalize(hh, qi)

    qTs = load_q(0)
    for hh in range(hp):
        reset(hh)
        scores(qTs, hh, 0, True)

    def q_body(qi, _):
        qTs = load_q(qi)
        for hh in range(hp):
            finish(hh, qi - 1, between=lambda: scores(qTs, hh, qi, True))
            reset(hh)

        def step(u, _):
            prev_tile = jnp.where(u <= 1, qi, u - 2)
            for hh in range(hp):
                pv(hh, prev_tile)
            for hh in range(hp):
                softmax(hh)
            for hh in range(hp):
                scores(qTs, hh, u, False)
            return 0

        lax.fori_loop(0, qi, step, 0)
        return 0

    lax.fori_loop(1, nq, q_body, 0)
    for hh in range(hp):
        finish(hh, jnp.int32(nq - 1))


def _fox(qf, kf, vf):
    B, H, R, S = qf.shape
    tq = min(TQ, S)
    hp = FOX_HP
    spec = pl.BlockSpec((1, hp, R, S), lambda b, h: (b, h, 0, 0))
    return pl.pallas_call(
        functools.partial(_fox_kernel, tq=tq),
        grid=(B, H // hp),
        in_specs=[spec, spec, spec],
        out_specs=pl.BlockSpec((1, hp * FOX_DH, S), lambda b, h: (b, h, 0)),
        out_shape=jax.ShapeDtypeStruct((B, FOX_W, S), BF16),
        scratch_shapes=[
            pltpu.VMEM((hp, tq, tq), F32),
            pltpu.VMEM((hp, tq, tq), BF16),
            pltpu.VMEM((hp, R, tq), F32),
            pltpu.VMEM((hp, 8, tq), F32),
            pltpu.VMEM((hp, 8, tq), F32),
            pltpu.VMEM((hp, 8, tq), F32),
        ],
        compiler_params=pltpu.CompilerParams(
            dimension_semantics=("arbitrary", "arbitrary"),
            vmem_limit_bytes=VMEM_LIMIT),
        name="fox",
    )(qf, kf, vf)


def _merge_kernel(x_ref, or_ref, of_ref, gmix_ref, wa_ref, wro_ref, wfo_ref, wout_ref, o_ref):
    x = x_ref[0]
    ms = jnp.mean(x * x, axis=-1, keepdims=True)
    h = (x * lax.rsqrt(ms + EPS) * gmix_ref[...]).astype(BF16)
    a = jnp.dot(h, wa_ref[...], preferred_element_type=F32)
    y_r = lax.dot_general(or_ref[0], wro_ref[...], TN_DIMS, preferred_element_type=F32)
    y_f = lax.dot_general(of_ref[0], wfo_ref[...], TN_DIMS, preferred_element_type=F32)
    merged = (jax.nn.sigmoid(a[:, :D_MODEL]) * y_r
              + jax.nn.sigmoid(a[:, D_MODEL:]) * y_f).astype(BF16)
    o_ref[0] = x + jnp.dot(merged, wout_ref[...], preferred_element_type=F32)


def _merge(x, o_r, o_f, gmix, wa, wro, wfo, wout):
    B, S, D = x.shape
    tm = min(TM_MERGE, S)
    const = lambda b, s: (0, 0)
    w = lambda shape: pl.BlockSpec(shape, const, pipeline_mode=pl.Buffered(1))
    return pl.pallas_call(
        _merge_kernel,
        grid=(B, S // tm),
        in_specs=[
            pl.BlockSpec((1, tm, D), lambda b, s: (b, s, 0)),
            pl.BlockSpec((1, RET_V, tm), lambda b, s: (b, 0, s)),
            pl.BlockSpec((1, FOX_W, tm), lambda b, s: (b, 0, s)),
            pl.BlockSpec((1, D), const),
            w((D, 2 * D)), w((RET_V, D)), w((FOX_W, D)), w((D, D)),
        ],
        out_specs=pl.BlockSpec((1, tm, D), lambda b, s: (b, s, 0)),
        out_shape=jax.ShapeDtypeStruct((B, S, D), F32),
        compiler_params=pltpu.CompilerParams(
            dimension_semantics=("arbitrary", "arbitrary"),
            vmem_limit_bytes=VMEM_LIMIT),
        name="merge",
    )(x, o_r, o_f, gmix, wa, wro, wfo, wout)


def _ffn_kernel(x_ref, g_ref, wg_ref, wu_ref, wd_ref, o_ref):
    x = x_ref[0]
    ms = jnp.mean(x * x, axis=-1, keepdims=True)
    h = (x * lax.rsqrt(ms + EPS) * g_ref[...]).astype(BF16)
    gate = jnp.dot(h, wg_ref[...], preferred_element_type=F32)
    up = jnp.dot(h, wu_ref[...], preferred_element_type=F32)
    act = (gate * jax.nn.sigmoid(gate) * up).astype(BF16)
    o_ref[0] = x + jnp.dot(act, wd_ref[...], preferred_element_type=F32)


def _ffn(x, g, wg, wu, wd):
    B, S, D = x.shape
    tm = min(TM_FFN, S)
    const = lambda b, s: (0, 0)
    w = lambda shape: pl.BlockSpec(shape, const, pipeline_mode=pl.Buffered(1))
    return pl.pallas_call(
        _ffn_kernel,
        grid=(B, S // tm),
        in_specs=[
            pl.BlockSpec((1, tm, D), lambda b, s: (b, s, 0)),
            pl.BlockSpec((1, D), const),
            w((D, D_FF)), w((D, D_FF)), w((D_FF, D)),
        ],
        out_specs=pl.BlockSpec((1, tm, D), lambda b, s: (b, s, 0)),
        out_shape=jax.ShapeDtypeStruct((B, S, D), F32),
        compiler_params=pltpu.CompilerParams(
            dimension_semantics=("arbitrary", "arbitrary"),
            vmem_limit_bytes=VMEM_LIMIT),
        name="ffn",
    )(x, g, wg, wu, wd)


def _rope_tables(S):
    half = RET_DK // 2
    pos = jnp.arange(S, dtype=F32)
    inv_freq = 1.0 / (ROPE_BASE ** (jnp.arange(half, dtype=F32) / half))
    ang = pos[:, None] * inv_freq[None, :]
    return jnp.cos(ang).T, jnp.sin(ang).T


def _retention_tables():
    log_g = jnp.log1p(-(2.0 ** (-5.0 - jnp.arange(RET_HEADS, dtype=F32))))
    idx = jnp.arange(CHUNK, dtype=F32)
    diff = idx[:, None] - idx[None, :]
    decay = jnp.where(diff[None] >= 0,
                      jnp.exp(jnp.maximum(diff, 0.0)[None] * log_g[:, None, None]), 0.0)
    decay_t = jnp.swapaxes(decay, 1, 2)
    zeta = jnp.exp((CHUNK - 1.0 - idx)[None, :] * log_g[:, None])[:, None, :]
    xi = jnp.exp((idx + 1.0)[None, :] * log_g[:, None])[:, None, :]
    gch = jnp.exp(CHUNK * log_g)[:, None, None]
    return decay_t, zeta, xi, gch


def kernel(x, g_mix, w_in, b_forget, g_ret_norm, w_ret_o, g_fox_q, g_fox_k, w_fox_o,
           w_out, g_ffn, w_gate, w_up, w_down):
    B, S, D = x.shape
    depth = g_mix.shape[0]
    cos_t, sin_t = _rope_tables(S)
    decay_t, zeta, xi, gch = _retention_tables()
    n_z = Z_FF + FOX_HEADS
    for l in range(depth):
        wt = jnp.pad(w_in[l][:, :n_z].T, ((0, Z_ROWS - n_z), (0, 0))).astype(BF16)
        wa = w_in[l][:, n_z:].astype(BF16)
        gmix = g_mix[l][None, :]
        qr, kr, vr, gr, qf, kf, vf = _inproj(
            x, gmix, wt, cos_t, sin_t,
            g_fox_q[l][:, None], g_fox_k[l][:, None], b_forget[l][:, None])
        o_r = _retention(qr, kr, vr, gr, decay_t, zeta, xi, gch,
                         g_ret_norm[l].reshape(RET_HEADS, RET_DV, 1))
        o_f = _fox(qf, kf, vf)
        x = _merge(x, o_r, o_f, gmix, wa, w_ret_o[l].astype(BF16), w_fox_o[l].astype(BF16),
                   w_out[l].astype(BF16))
        x = _ffn(x, g_ffn[l][None, :], w_gate[l].astype(BF16), w_up[l].astype(BF16),
                 w_down[l].astype(BF16))
    return x
```

```python
import functools
import math

import jax
import jax.numpy as jnp
from jax import lax
from jax.experimental import pallas as pl
from jax.experimental.pallas import tpu as pltpu

F32 = jnp.float32
BF16 = jnp.bfloat16

D_MODEL = 1024
RET_HEADS = 4
RET_DK = 64
RET_DV = 128
RET_QK = RET_HEADS * RET_DK
RET_V = RET_HEADS * RET_DV
CHUNK = 128
FOX_HEADS = 8
FOX_DH = 64
FOX_W = FOX_HEADS * FOX_DH
D_FF = 2816
ROPE_BASE = 10000.0
EPS = 1e-6

Z_QR, Z_KR, Z_VR, Z_GR = 0, 256, 512, 1024
Z_QF, Z_KF, Z_VF, Z_FF = 1536, 2048, 2560, 3072
Z_ROWS = 3088
AUG = 16
FOX_ROWS = FOX_DH + AUG
NEG = -1e30

TM_IN = 512
TS_RET = 1024
RET_UNROLL = 8
TQ = 1024
FOX_HP = 1
LOG2E = math.log2(math.e)
TM_MERGE = 512
TM_FFN = 512
VMEM_LIMIT = 56 * 1024 * 1024

NT_DIMS = (((1,), (1,)), ((), ()))
TN_DIMS = (((0,), (0,)), ((), ()))


def _split3(c):
    hi = c.astype(BF16).astype(F32)
    r = c - hi
    mid = r.astype(BF16).astype(F32)
    lo = (r - mid).astype(BF16).astype(F32)
    return hi, mid, lo


def _inproj_kernel(x_ref, gmix_ref, wt_ref, cos_ref, sin_ref, gq_ref, gk_ref, bf_ref,
                   qr_ref, kr_ref, vr_ref, gr_ref, qf_ref, kf_ref, vf_ref,
                   carry_ref):
    tm = x_ref.shape[1]

    @pl.when(pl.program_id(1) == 0)
    def _():
        carry_ref[...] = jnp.zeros_like(carry_ref)

    x = x_ref[0]
    ms = jnp.mean(x * x, axis=-1, keepdims=True)
    h = (x * lax.rsqrt(ms + EPS) * gmix_ref[...]).astype(BF16)
    groups = ((Z_QF, Z_ROWS), (Z_VR, Z_QF), (Z_QR, Z_VR))
    parts = [(a, b, lax.dot_general(wt_ref[a:b, :], h, NT_DIMS, preferred_element_type=F32))
             for a, b in groups]

    class _Rows:
        def __getitem__(self, sl):
            for a, b, part in parts:
                if a <= sl.start and sl.stop <= b:
                    return part[sl.start - a:sl.stop - a]
            raise IndexError(sl)

    z = _Rows()

    u = z[Z_FF:Z_FF + FOX_HEADS] + bf_ref[...]
    ls = jnp.minimum(u, 0.0) - jnp.log1p(jnp.exp(-jnp.abs(u)))
    lane = lax.broadcasted_iota(jnp.int32, ls.shape, 1)
    c = ls
    d = 1
    while d < tm:
        c = c + jnp.where(lane >= d, pltpu.roll(c, d, axis=1), 0.0)
        d *= 2
    c = c + carry_ref[:, 0:1]
    carry_ref[...] = jnp.broadcast_to(c[:, tm - 1:tm], carry_ref.shape)

    ridx = lax.broadcasted_iota(jnp.int32, (AUG, tm), 0)
    scale = FOX_DH ** -0.5 * LOG2E
    c2 = c * LOG2E
    for hh in range(FOX_HEADS):
        r0 = hh * FOX_DH
        q = z[Z_QF + r0:Z_QF + r0 + FOX_DH]
        k = z[Z_KF + r0:Z_KF + r0 + FOX_DH]
        qn = q * lax.rsqrt(jnp.mean(q * q, axis=0, keepdims=True) + EPS) * gq_ref[...]
        kn = k * lax.rsqrt(jnp.mean(k * k, axis=0, keepdims=True) + EPS) * gk_ref[...]
        c_hi, c_mid, c_lo = _split3(c2[hh:hh + 1, :])
        q_aug = jnp.where(ridx == 0, c_hi, jnp.where(ridx == 1, c_mid,
                          jnp.where(ridx == 2, c_lo, jnp.where(ridx < 6, 1.0, 0.0))))
        k_aug = jnp.where(ridx == 3, -c_hi, jnp.where(ridx == 4, -c_mid,
                          jnp.where(ridx == 5, -c_lo, jnp.where(ridx < 3, 1.0, 0.0))))
        qf_ref[0, hh, 0:FOX_DH, :] = (qn * scale).astype(BF16)
        qf_ref[0, hh, FOX_DH:FOX_ROWS, :] = q_aug.astype(BF16)
        kf_ref[0, hh, 0:FOX_DH, :] = kn.astype(BF16)
        kf_ref[0, hh, FOX_DH:FOX_ROWS, :] = k_aug.astype(BF16)
        vf_ref[0, hh, 0:FOX_DH, :] = z[Z_VF + r0:Z_VF + r0 + FOX_DH].astype(BF16)
        vf_ref[0, hh, FOX_DH:FOX_ROWS, :] = jnp.where(ridx == 0, 1.0, 0.0).astype(BF16)

    vr_ref[0] = z[Z_VR:Z_VR + RET_V].astype(BF16)
    g = z[Z_GR:Z_GR + RET_V]
    gr_ref[0] = (g * jax.nn.sigmoid(g)).astype(BF16)

    cos = cos_ref[...]
    sin = sin_ref[...]
    half = RET_DK // 2
    for hh in range(RET_HEADS):
        for base, ref, scl in ((Z_QR, qr_ref, 1.0), (Z_KR, kr_ref, RET_DK ** -0.5)):
            r0 = base + hh * RET_DK
            x1 = z[r0:r0 + half]
            x2 = z[r0 + half:r0 + RET_DK]
            o1 = (x1 * cos - x2 * sin) * scl
            o2 = (x1 * sin + x2 * cos) * scl
            ref[0, hh * RET_DK:hh * RET_DK + half, :] = o1.astype(BF16)
            ref[0, hh * RET_DK + half:(hh + 1) * RET_DK, :] = o2.astype(BF16)


def _inproj(x, gmix, wt, cos_t, sin_t, gq, gk, bf):
    B, S, D = x.shape
    tm = min(TM_IN, S)
    grid = (B, S // tm)
    const = lambda b, s: (0, 0)
    fm = lambda rows: pl.BlockSpec((1, rows, tm), lambda b, s: (b, 0, s))
    fox = pl.BlockSpec((1, FOX_HEADS, FOX_ROWS, tm), lambda b, s: (b, 0, 0, s))
    out_shape = (
        jax.ShapeDtypeStruct((B, RET_QK, S), BF16),
        jax.ShapeDtypeStruct((B, RET_QK, S), BF16),
        jax.ShapeDtypeStruct((B, RET_V, S), BF16),
        jax.ShapeDtypeStruct((B, RET_V, S), BF16),
        jax.ShapeDtypeStruct((B, FOX_HEADS, FOX_ROWS, S), BF16),
        jax.ShapeDtypeStruct((B, FOX_HEADS, FOX_ROWS, S), BF16),
        jax.ShapeDtypeStruct((B, FOX_HEADS, FOX_ROWS, S), BF16),
    )
    return pl.pallas_call(
        _inproj_kernel,
        grid=grid,
        in_specs=[
            pl.BlockSpec((1, tm, D), lambda b, s: (b, s, 0)),
            pl.BlockSpec((1, D), const),
            pl.BlockSpec((Z_ROWS, D), const, pipeline_mode=pl.Buffered(1)),
            pl.BlockSpec((RET_DK // 2, tm), lambda b, s: (0, s)),
            pl.BlockSpec((RET_DK // 2, tm), lambda b, s: (0, s)),
            pl.BlockSpec((FOX_DH, 1), const),
            pl.BlockSpec((FOX_DH, 1), const),
            pl.BlockSpec((FOX_HEADS, 1), const),
        ],
        out_specs=(fm(RET_QK), fm(RET_QK), fm(RET_V), fm(RET_V), fox, fox, fox),
        out_shape=out_shape,
        scratch_shapes=[pltpu.VMEM((FOX_HEADS, 128), F32)],
        compiler_params=pltpu.CompilerParams(
            dimension_semantics=("arbitrary", "arbitrary"),
            vmem_limit_bytes=VMEM_LIMIT),
        name="inproj",
    )(x, gmix, wt, cos_t, sin_t, gq, gk, bf)


def _retention_kernel(q_ref, k_ref, v_ref, g_ref, decay_ref, zeta_ref, xi_ref, gch_ref, gn_ref,
                      o_ref, st_ref):
    ts = q_ref.shape[2]

    @pl.when(pl.program_id(1) == 0)
    def _():
        st_ref[...] = jnp.zeros_like(st_ref)

    def chunk_body(ci, _):
        t0 = pl.multiple_of(ci * CHUNK, CHUNK)
        heads = range(RET_HEADS)
        qT = [q_ref[0, hh * RET_DK:(hh + 1) * RET_DK, pl.ds(t0, CHUNK)] for hh in heads]
        kT = [k_ref[0, hh * RET_DK:(hh + 1) * RET_DK, pl.ds(t0, CHUNK)] for hh in heads]
        vT = [v_ref[0, hh * RET_DV:(hh + 1) * RET_DV, pl.ds(t0, CHUNK)] for hh in heads]
        st = [st_ref[hh] for hh in heads]
        sT = [lax.dot_general(kT[hh], qT[hh], TN_DIMS, preferred_element_type=F32) for hh in heads]
        inter = [lax.dot_general(st[hh].astype(BF16), qT[hh], TN_DIMS,
                                 preferred_element_type=F32) for hh in heads]
        for hh in heads:
            vz = (vT[hh].astype(F32) * zeta_ref[hh]).astype(BF16)
            kv = lax.dot_general(kT[hh], vz, NT_DIMS, preferred_element_type=F32)
            st_ref[hh] = gch_ref[hh] * st[hh] + kv
        for hh in heads:
            pT = (sT[hh] * decay_ref[hh]).astype(BF16)
            intra = jnp.dot(vT[hh], pT, preferred_element_type=F32)
            o = intra + inter[hh] * xi_ref[hh]
            mu = jnp.mean(o, axis=0, keepdims=True)
            oc = o - mu
            var = jnp.mean(oc * oc, axis=0, keepdims=True)
            on = oc * lax.rsqrt(var + EPS) * gn_ref[hh]
            sg = g_ref[0, hh * RET_DV:(hh + 1) * RET_DV, pl.ds(t0, CHUNK)]
            o_ref[0, hh * RET_DV:(hh + 1) * RET_DV, pl.ds(t0, CHUNK)] = (
                sg.astype(F32) * on).astype(BF16)
        return 0

    lax.fori_loop(0, ts // CHUNK, chunk_body, 0, unroll=RET_UNROLL)


def _retention(qr, kr, vr, gr, decay_t, zeta, xi, gch, gn):
    B, _, S = qr.shape
    ts = min(TS_RET, S)
    grid = (B, S // ts)
    fm = lambda rows: pl.BlockSpec((1, rows, ts), lambda b, s: (b, 0, s))
    c3 = lambda b, s: (0, 0, 0)
    return pl.pallas_call(
        _retention_kernel,
        grid=grid,
        in_specs=[
            fm(RET_QK), fm(RET_QK), fm(RET_V), fm(RET_V),
            pl.BlockSpec((RET_HEADS, CHUNK, CHUNK), c3),
            pl.BlockSpec((RET_HEADS, 1, CHUNK), c3),
            pl.BlockSpec((RET_HEADS, 1, CHUNK), c3),
            pl.BlockSpec((RET_HEADS, 1, 1), c3),
            pl.BlockSpec((RET_HEADS, RET_DV, 1), c3),
        ],
        out_specs=fm(RET_V),
        out_shape=jax.ShapeDtypeStruct((B, RET_V, S), BF16),
        scratch_shapes=[pltpu.VMEM((RET_HEADS, RET_DK, RET_DV), F32)],
        compiler_params=pltpu.CompilerParams(
            dimension_semantics=("arbitrary", "arbitrary"),
            vmem_limit_bytes=VMEM_LIMIT),
        name="retention",
    )(qr, kr, vr, gr, decay_t, zeta, xi, gch, gn)


def _fox_kernel(q_ref, k_ref, v_ref, o_ref, s_scr, p_scr, acc_scr, m_scr, al_scr, mx_scr, *, tq):
    hp = q_ref.shape[1]
    S = q_ref.shape[3]
    tk = tq
    nq = S // tq

    def load_q(qi):
        q0 = pl.multiple_of(qi * tq, tq)
        return [q_ref[0, hh, :, pl.ds(q0, tq)] for hh in range(hp)]

    def scores(qTs, hh, tile, masked):
        k0 = pl.multiple_of(tile * tk, tk)
        kT = k_ref[0, hh, :, pl.ds(k0, tk)]
        s = lax.dot_general(kT, qTs[hh], TN_DIMS, preferred_element_type=F32)
        if masked:
            row = lax.broadcasted_iota(jnp.int32, (tk, tq), 0)
            col = lax.broadcasted_iota(jnp.int32, (tk, tq), 1)
            s = jnp.where(row > col, NEG, s)
        s_scr[hh] = s
        mx_scr[hh] = jnp.broadcast_to(jnp.max(s, axis=0, keepdims=True), (8, tq))

    def pv(hh, tile):
        k0 = pl.multiple_of(tile * tk, tk)
        vT = v_ref[0, hh, :, pl.ds(k0, tk)]
        acc_scr[hh] = (acc_scr[hh] * al_scr[hh][0:1]
                       + jnp.dot(vT, p_scr[hh], preferred_element_type=F32))

    def softmax(hh):
        m = m_scr[hh]
        m_new = jnp.maximum(m, mx_scr[hh])
        p_scr[hh] = jnp.exp2(s_scr[hh] - m_new[0:1]).astype(BF16)
        al_scr[hh] = jnp.exp2(m - m_new)
        m_scr[hh] = m_new

    def reset(hh):
        p_scr[hh] = jnp.zeros((tk, tq), BF16)
        acc_scr[hh] = jnp.zeros((FOX_ROWS, tq), F32)
        m_scr[hh] = jnp.full((8, tq), NEG, F32)
        al_scr[hh] = jnp.ones((8, tq), F32)

    def finalize(hh, qi):
        q0 = pl.multiple_of(qi * tq, tq)
        acc = acc_scr[hh]
        o = acc[0:FOX_DH] / acc[FOX_DH:FOX_DH + 1]
        o_ref[0, hh * FOX_DH:(hh + 1) * FOX_DH, pl.ds(q0, tq)] = o.astype(BF16)

    def finish(hh, qi, between=None):
        pv(hh, jnp.where(qi <= 1, qi, qi - 2))
        softmax(hh)
        if between is not None:
            between()
        pv(hh, jnp.maximum(qi - 1, 0))
        finalize(hh, qi)

    qTs = load_q(0)
    for hh in range(hp):
        reset(hh)
        scores(qTs, hh, 0, True)

    def q_body(qi, _):
        qTs = load_q(qi)
        for hh in range(hp):
            finish(hh, qi - 1, between=lambda: scores(qTs, hh, qi, True))
            reset(hh)

        def step(u, _):
            prev_tile = jnp.where(u <= 1, qi, u - 2)
            for hh in range(hp):
                pv(hh, prev_tile)
            for hh in range(hp):
                softmax(hh)
            for hh in range(hp):
                scores(qTs, hh, u, False)
            return 0

        lax.fori_loop(0, qi, step, 0)
        return 0

    lax.fori_loop(1, nq, q_body, 0)
    for hh in range(hp):
        finish(hh, jnp.int32(nq - 1))


def _fox(qf, kf, vf):
    B, H, R, S = qf.shape
    tq = min(TQ, S)
    hp = FOX_HP
    spec = pl.BlockSpec((1, hp, R, S), lambda b, h: (b, h, 0, 0))
    return pl.pallas_call(
        functools.partial(_fox_kernel, tq=tq),
        grid=(B, H // hp),
        in_specs=[spec, spec, spec],
        out_specs=pl.BlockSpec((1, hp * FOX_DH, S), lambda b, h: (b, h, 0)),
        out_shape=jax.ShapeDtypeStruct((B, FOX_W, S), BF16),
        scratch_shapes=[
            pltpu.VMEM((hp, tq, tq), F32),
            pltpu.VMEM((hp, tq, tq), BF16),
            pltpu.VMEM((hp, R, tq), F32),
            pltpu.VMEM((hp, 8, tq), F32),
            pltpu.VMEM((hp, 8, tq), F32),
            pltpu.VMEM((hp, 8, tq), F32),
        ],
        compiler_params=pltpu.CompilerParams(
            dimension_semantics=("arbitrary", "arbitrary"),
            vmem_limit_bytes=VMEM_LIMIT),
        name="fox",
    )(qf, kf, vf)


def _merge_kernel(x_ref, or_ref, of_ref, gmix_ref, wa_ref, wro_ref, wfo_ref, wout_ref, o_ref):
    x = x_ref[0]
    ms = jnp.mean(x * x, axis=-1, keepdims=True)
    h = (x * lax.rsqrt(ms + EPS) * gmix_ref[...]).astype(BF16)
    a = jnp.dot(h, wa_ref[...], preferred_element_type=F32)
    y_r = lax.dot_general(or_ref[0], wro_ref[...], TN_DIMS, preferred_element_type=F32)
    y_f = lax.dot_general(of_ref[0], wfo_ref[...], TN_DIMS, preferred_element_type=F32)
    merged = (jax.nn.sigmoid(a[:, :D_MODEL]) * y_r
              + jax.nn.sigmoid(a[:, D_MODEL:]) * y_f).astype(BF16)
    o_ref[0] = x + jnp.dot(merged, wout_ref[...], preferred_element_type=F32)


def _merge(x, o_r, o_f, gmix, wa, wro, wfo, wout):
    B, S, D = x.shape
    tm = min(TM_MERGE, S)
    const = lambda b, s: (0, 0)
    w = lambda shape: pl.BlockSpec(shape, const, pipeline_mode=pl.Buffered(1))
    return pl.pallas_call(
        _merge_kernel,
        grid=(B, S // tm),
        in_specs=[
            pl.BlockSpec((1, tm, D), lambda b, s: (b, s, 0)),
            pl.BlockSpec((1, RET_V, tm), lambda b, s: (b, 0, s)),
            pl.BlockSpec((1, FOX_W, tm), lambda b, s: (b, 0, s)),
            pl.BlockSpec((1, D), const),
            w((D, 2 * D)), w((RET_V, D)), w((FOX_W, D)), w((D, D)),
        ],
        out_specs=pl.BlockSpec((1, tm, D), lambda b, s: (b, s, 0)),
        out_shape=jax.ShapeDtypeStruct((B, S, D), F32),
        compiler_params=pltpu.CompilerParams(
            dimension_semantics=("arbitrary", "arbitrary"),
            vmem_limit_bytes=VMEM_LIMIT),
        name="merge",
    )(x, o_r, o_f, gmix, wa, wro, wfo, wout)


def _ffn_kernel(x_ref, g_ref, wg_ref, wu_ref, wd_ref, o_ref):
    x = x_ref[0]
    ms = jnp.mean(x * x, axis=-1, keepdims=True)
    h = (x * lax.rsqrt(ms + EPS) * g_ref[...]).astype(BF16)
    gate = jnp.dot(h, wg_ref[...], preferred_element_type=F32)
    up = jnp.dot(h, wu_ref[...], preferred_element_type=F32)
    act = (gate * jax.nn.sigmoid(gate) * up).astype(BF16)
    o_ref[0] = x + jnp.dot(act, wd_ref[...], preferred_element_type=F32)


def _ffn(x, g, wg, wu, wd):
    B, S, D = x.shape
    tm = min(TM_FFN, S)
    const = lambda b, s: (0, 0)
    w = lambda shape: pl.BlockSpec(shape, const, pipeline_mode=pl.Buffered(1))
    return pl.pallas_call(
        _ffn_kernel,
        grid=(B, S // tm),
        in_specs=[
            pl.BlockSpec((1, tm, D), lambda b, s: (b, s, 0)),
            pl.BlockSpec((1, D), const),
            w((D, D_FF)), w((D, D_FF)), w((D_FF, D)),
        ],
        out_specs=pl.BlockSpec((1, tm, D), lambda b, s: (b, s, 0)),
        out_shape=jax.ShapeDtypeStruct((B, S, D), F32),
        compiler_params=pltpu.CompilerParams(
            dimension_semantics=("arbitrary", "arbitrary"),
            vmem_limit_bytes=VMEM_LIMIT),
        name="ffn",
    )(x, g, wg, wu, wd)


def _rope_tables(S):
    half = RET_DK // 2
    pos = jnp.arange(S, dtype=F32)
    inv_freq = 1.0 / (ROPE_BASE ** (jnp.arange(half, dtype=F32) / half))
    ang = pos[:, None] * inv_freq[None, :]
    return jnp.cos(ang).T, jnp.sin(ang).T


def _retention_tables():
    log_g = jnp.log1p(-(2.0 ** (-5.0 - jnp.arange(RET_HEADS, dtype=F32))))
    idx = jnp.arange(CHUNK, dtype=F32)
    diff = idx[:, None] - idx[None, :]
    decay = jnp.where(diff[None] >= 0,
                      jnp.exp(jnp.maximum(diff, 0.0)[None] * log_g[:, None, None]), 0.0)
    decay_t = jnp.swapaxes(decay, 1, 2)
    zeta = jnp.exp((CHUNK - 1.0 - idx)[None, :] * log_g[:, None])[:, None, :]
    xi = jnp.exp((idx + 1.0)[None, :] * log_g[:, None])[:, None, :]
    gch = jnp.exp(CHUNK * log_g)[:, None, None]
    return decay_t, zeta, xi, gch


def kernel(x, g_mix, w_in, b_forget, g_ret_norm, w_ret_o, g_fox_q, g_fox_k, w_fox_o,
           w_out, g_ffn, w_gate, w_up, w_down):
    B, S, D = x.shape
    depth = g_mix.shape[0]
    cos_t, sin_t = _rope_tables(S)
    decay_t, zeta, xi, gch = _retention_tables()
    n_z = Z_FF + FOX_HEADS
    for l in range(depth):
        wt = jnp.pad(w_in[l][:, :n_z].T, ((0, Z_ROWS - n_z), (0, 0))).astype(BF16)
        wa = w_in[l][:, n_z:].astype(BF16)
        gmix = g_mix[l][None, :]
        qr, kr, vr, gr, qf, kf, vf = _inproj(
            x, gmix, wt, cos_t, sin_t,
            g_fox_q[l][:, None], g_fox_k[l][:, None], b_forget[l][:, None])
        o_r = _retention(qr, kr, vr, gr, decay_t, zeta, xi, gch,
                         g_ret_norm[l].reshape(RET_HEADS, RET_DV, 1))
        o_f = _fox(qf, kf, vf)
        x = _merge(x, o_r, o_f, gmix, wa, w_ret_o[l].astype(BF16), w_fox_o[l].astype(BF16),
                   w_out[l].astype(BF16))
        x = _ffn(x, g_ffn[l][None, :], w_gate[l].astype(BF16), w_up[l].astype(BF16),
                 w_down[l].astype(BF16))
    return x
```

```python
import functools
import math

import jax
import jax.numpy as jnp
from jax import lax
from jax.experimental import pallas as pl
from jax.experimental.pallas import tpu as pltpu

F32 = jnp.float32
BF16 = jnp.bfloat16

D_MODEL = 1024
RET_HEADS = 4
RET_DK = 64
RET_DV = 128
RET_QK = RET_HEADS * RET_DK
RET_V = RET_HEADS * RET_DV
CHUNK = 128
FOX_HEADS = 8
FOX_DH = 64
FOX_W = FOX_HEADS * FOX_DH
D_FF = 2816
ROPE_BASE = 10000.0
EPS = 1e-6

Z_QR, Z_KR, Z_VR, Z_GR = 0, 256, 512, 1024
Z_QF, Z_KF, Z_VF, Z_FF = 1536, 2048, 2560, 3072
Z_ROWS = 3088
AUG = 16
FOX_ROWS = FOX_DH + AUG
NEG = -1e30

TM_IN = 512
TS_RET = 1024
RET_UNROLL = 8
TQ = 1024
FOX_HP = 2
LOG2E = math.log2(math.e)
TM_MERGE = 512
TM_FFN = 512
VMEM_LIMIT = 56 * 1024 * 1024

NT_DIMS = (((1,), (1,)), ((), ()))
TN_DIMS = (((0,), (0,)), ((), ()))


def _split3(c):
    hi = c.astype(BF16).astype(F32)
    r = c - hi
    mid = r.astype(BF16).astype(F32)
    lo = (r - mid).astype(BF16).astype(F32)
    return hi, mid, lo


def _inproj_kernel(x_ref, gmix_ref, wt_ref, cos_ref, sin_ref, gq_ref, gk_ref, bf_ref,
                   qr_ref, kr_ref, vr_ref, gr_ref, qf_ref, kf_ref, vf_ref,
                   carry_ref):
    tm = x_ref.shape[1]

    @pl.when(pl.program_id(1) == 0)
    def _():
        carry_ref[...] = jnp.zeros_like(carry_ref)

    x = x_ref[0]
    ms = jnp.mean(x * x, axis=-1, keepdims=True)
    h = (x * lax.rsqrt(ms + EPS) * gmix_ref[...]).astype(BF16)
    groups = ((Z_QF, Z_ROWS), (Z_VR, Z_QF), (Z_QR, Z_VR))
    parts = [(a, b, lax.dot_general(wt_ref[a:b, :], h, NT_DIMS, preferred_element_type=F32))
             for a, b in groups]

    class _Rows:
        def __getitem__(self, sl):
            for a, b, part in parts:
                if a <= sl.start and sl.stop <= b:
                    return part[sl.start - a:sl.stop - a]
            raise IndexError(sl)

    z = _Rows()

    u = z[Z_FF:Z_FF + FOX_HEADS] + bf_ref[...]
    ls = jnp.minimum(u, 0.0) - jnp.log1p(jnp.exp(-jnp.abs(u)))
    lane = lax.broadcasted_iota(jnp.int32, ls.shape, 1)
    c = ls
    d = 1
    while d < tm:
        c = c + jnp.where(lane >= d, pltpu.roll(c, d, axis=1), 0.0)
        d *= 2
    c = c + carry_ref[:, 0:1]
    carry_ref[...] = jnp.broadcast_to(c[:, tm - 1:tm], carry_ref.shape)

    ridx = lax.broadcasted_iota(jnp.int32, (AUG, tm), 0)
    scale = FOX_DH ** -0.5 * LOG2E
    c2 = c * LOG2E
    for hh in range(FOX_HEADS):
        r0 = hh * FOX_DH
        q = z[Z_QF + r0:Z_QF + r0 + FOX_DH]
        k = z[Z_KF + r0:Z_KF + r0 + FOX_DH]
        qn = q * lax.rsqrt(jnp.mean(q * q, axis=0, keepdims=True) + EPS) * gq_ref[...]
        kn = k * lax.rsqrt(jnp.mean(k * k, axis=0, keepdims=True) + EPS) * gk_ref[...]
        c_hi, c_mid, c_lo = _split3(c2[hh:hh + 1, :])
        q_aug = jnp.where(ridx == 0, c_hi, jnp.where(ridx == 1, c_mid,
                          jnp.where(ridx == 2, c_lo, jnp.where(ridx < 6, 1.0, 0.0))))
        k_aug = jnp.where(ridx == 3, -c_hi, jnp.where(ridx == 4, -c_mid,
                          jnp.where(ridx == 5, -c_lo, jnp.where(ridx < 3, 1.0, 0.0))))
        qf_ref[0, hh, 0:FOX_DH, :] = (qn * scale).astype(BF16)
        qf_ref[0, hh, FOX_DH:FOX_ROWS, :] = q_aug.astype(BF16)
        kf_ref[0, hh, 0:FOX_DH, :] = kn.astype(BF16)
        kf_ref[0, hh, FOX_DH:FOX_ROWS, :] = k_aug.astype(BF16)
        vf_ref[0, hh, 0:FOX_DH, :] = z[Z_VF + r0:Z_VF + r0 + FOX_DH].astype(BF16)
        vf_ref[0, hh, FOX_DH:FOX_ROWS, :] = jnp.where(ridx == 0, 1.0, 0.0).astype(BF16)

    vr_ref[0] = z[Z_VR:Z_VR + RET_V].astype(BF16)
    g = z[Z_GR:Z_GR + RET_V]
    gr_ref[0] = (g * jax.nn.sigmoid(g)).astype(BF16)

    cos = cos_ref[...]
    sin = sin_ref[...]
    half = RET_DK // 2
    for hh in range(RET_HEADS):
        for base, ref, scl in ((Z_QR, qr_ref, 1.0), (Z_KR, kr_ref, RET_DK ** -0.5)):
            r0 = base + hh * RET_DK
            x1 = z[r0:r0 + half]
            x2 = z[r0 + half:r0 + RET_DK]
            o1 = (x1 * cos - x2 * sin) * scl
            o2 = (x1 * sin + x2 * cos) * scl
            ref[0, hh * RET_DK:hh * RET_DK + half, :] = o1.astype(BF16)
            ref[0, hh * RET_DK + half:(hh + 1) * RET_DK, :] = o2.astype(BF16)


def _inproj(x, gmix, wt, cos_t, sin_t, gq, gk, bf):
    B, S, D = x.shape
    tm = min(TM_IN, S)
    grid = (B, S // tm)
    const = lambda b, s: (0, 0)
    fm = lambda rows: pl.BlockSpec((1, rows, tm), lambda b, s: (b, 0, s))
    fox = pl.BlockSpec((1, FOX_HEADS, FOX_ROWS, tm), lambda b, s: (b, 0, 0, s))
    out_shape = (
        jax.ShapeDtypeStruct((B, RET_QK, S), BF16),
        jax.ShapeDtypeStruct((B, RET_QK, S), BF16),
        jax.ShapeDtypeStruct((B, RET_V, S), BF16),
        jax.ShapeDtypeStruct((B, RET_V, S), BF16),
        jax.ShapeDtypeStruct((B, FOX_HEADS, FOX_ROWS, S), BF16),
        jax.ShapeDtypeStruct((B, FOX_HEADS, FOX_ROWS, S), BF16),
        jax.ShapeDtypeStruct((B, FOX_HEADS, FOX_ROWS, S), BF16),
    )
    return pl.pallas_call(
        _inproj_kernel,
        grid=grid,
        in_specs=[
            pl.BlockSpec((1, tm, D), lambda b, s: (b, s, 0)),
            pl.BlockSpec((1, D), const),
            pl.BlockSpec((Z_ROWS, D), const, pipeline_mode=pl.Buffered(1)),
            pl.BlockSpec((RET_DK // 2, tm), lambda b, s: (0, s)),
            pl.BlockSpec((RET_DK // 2, tm), lambda b, s: (0, s)),
            pl.BlockSpec((FOX_DH, 1), const),
            pl.BlockSpec((FOX_DH, 1), const),
            pl.BlockSpec((FOX_HEADS, 1), const),
        ],
        out_specs=(fm(RET_QK), fm(RET_QK), fm(RET_V), fm(RET_V), fox, fox, fox),
        out_shape=out_shape,
        scratch_shapes=[pltpu.VMEM((FOX_HEADS, 128), F32)],
        compiler_params=pltpu.CompilerParams(
            dimension_semantics=("arbitrary", "arbitrary"),
            vmem_limit_bytes=VMEM_LIMIT),
        name="inproj",
    )(x, gmix, wt, cos_t, sin_t, gq, gk, bf)


def _retention_kernel(q_ref, k_ref, v_ref, g_ref, decay_ref, zeta_ref, xi_ref, gch_ref, gn_ref,
                      o_ref, st_ref):
    ts = q_ref.shape[2]

    @pl.when(pl.program_id(1) == 0)
    def _():
        st_ref[...] = jnp.zeros_like(st_ref)

    def chunk_body(ci, _):
        t0 = pl.multiple_of(ci * CHUNK, CHUNK)
        heads = range(RET_HEADS)
        qT = [q_ref[0, hh * RET_DK:(hh + 1) * RET_DK, pl.ds(t0, CHUNK)] for hh in heads]
        kT = [k_ref[0, hh * RET_DK:(hh + 1) * RET_DK, pl.ds(t0, CHUNK)] for hh in heads]
        vT = [v_ref[0, hh * RET_DV:(hh + 1) * RET_DV, pl.ds(t0, CHUNK)] for hh in heads]
        st = [st_ref[hh] for hh in heads]
        sT = [lax.dot_general(kT[hh], qT[hh], TN_DIMS, preferred_element_type=F32) for hh in heads]
        inter = [lax.dot_general(st[hh].astype(BF16), qT[hh], TN_DIMS,
                                 preferred_element_type=F32) for hh in heads]
        for hh in heads:
            vz = (vT[hh].astype(F32) * zeta_ref[hh]).astype(BF16)
            kv = lax.dot_general(kT[hh], vz, NT_DIMS, preferred_element_type=F32)
            st_ref[hh] = gch_ref[hh] * st[hh] + kv
        for hh in heads:
            pT = (sT[hh] * decay_ref[hh]).astype(BF16)
            intra = jnp.dot(vT[hh], pT, preferred_element_type=F32)
            o = intra + inter[hh] * xi_ref[hh]
            mu = jnp.mean(o, axis=0, keepdims=True)
            oc = o - mu
            var = jnp.mean(oc * oc, axis=0, keepdims=True)
            on = oc * lax.rsqrt(var + EPS) * gn_ref[hh]
            sg = g_ref[0, hh * RET_DV:(hh + 1) * RET_DV, pl.ds(t0, CHUNK)]
            o_ref[0, hh * RET_DV:(hh + 1) * RET_DV, pl.ds(t0, CHUNK)] = (
                sg.astype(F32) * on).astype(BF16)
        return 0

    lax.fori_loop(0, ts // CHUNK, chunk_body, 0, unroll=RET_UNROLL)


def _retention(qr, kr, vr, gr, decay_t, zeta, xi, gch, gn):
    B, _, S = qr.shape
    ts = min(TS_RET, S)
    grid = (B, S // ts)
    fm = lambda rows: pl.BlockSpec((1, rows, ts), lambda b, s: (b, 0, s))
    c3 = lambda b, s: (0, 0, 0)
    return pl.pallas_call(
        _retention_kernel,
        grid=grid,
        in_specs=[
            fm(RET_QK), fm(RET_QK), fm(RET_V), fm(RET_V),
            pl.BlockSpec((RET_HEADS, CHUNK, CHUNK), c3),
            pl.BlockSpec((RET_HEADS, 1, CHUNK), c3),
            pl.BlockSpec((RET_HEADS, 1, CHUNK), c3),
            pl.BlockSpec((RET_HEADS, 1, 1), c3),
            pl.BlockSpec((RET_HEADS, RET_DV, 1), c3),
        ],
        out_specs=fm(RET_V),
        out_shape=jax.ShapeDtypeStruct((B, RET_V, S), BF16),
        scratch_shapes=[pltpu.VMEM((RET_HEADS, RET_DK, RET_DV), F32)],
        compiler_params=pltpu.CompilerParams(
            dimension_semantics=("arbitrary", "arbitrary"),
            vmem_limit_bytes=VMEM_LIMIT),
        name="retention",
    )(qr, kr, vr, gr, decay_t, zeta, xi, gch, gn)


def _fox_kernel(*refs, tq, hp):
    q_refs, k_refs, v_refs = refs[0:hp], refs[hp:2 * hp], refs[2 * hp:3 * hp]
    o_ref = refs[3 * hp]
    s_scr, p_scr, acc_scr, m_scr, al_scr, mx_scr = (
        refs[3 * hp + 1 + i * hp:3 * hp + 1 + (i + 1) * hp] for i in range(6))
    S = q_refs[0].shape[3]
    tk = tq
    nq = S // tq

    def load_q(qi):
        q0 = pl.multiple_of(qi * tq, tq)
        return [q_refs[hh][0, 0, :, pl.ds(q0, tq)] for hh in range(hp)]

    def scores(qTs, hh, tile, masked):
        k0 = pl.multiple_of(tile * tk, tk)
        kT = k_refs[hh][0, 0, :, pl.ds(k0, tk)]
        s = lax.dot_general(kT, qTs[hh], TN_DIMS, preferred_element_type=F32)
        if masked:
            row = lax.broadcasted_iota(jnp.int32, (tk, tq), 0)
            col = lax.broadcasted_iota(jnp.int32, (tk, tq), 1)
            s = jnp.where(row > col, NEG, s)
        s_scr[hh][...] = s
        mx_scr[hh][...] = jnp.broadcast_to(jnp.max(s, axis=0, keepdims=True), (8, tq))

    def pv(hh, tile):
        k0 = pl.multiple_of(tile * tk, tk)
        vT = v_refs[hh][0, 0, :, pl.ds(k0, tk)]
        acc_scr[hh][...] = (acc_scr[hh][...] * al_scr[hh][...][0:1]
                       + jnp.dot(vT, p_scr[hh][...], preferred_element_type=F32))

    def softmax(hh):
        m = m_scr[hh][...]
        m_new = jnp.maximum(m, mx_scr[hh][...])
        p_scr[hh][...] = jnp.exp2(s_scr[hh][...] - m_new[0:1]).astype(BF16)
        al_scr[hh][...] = jnp.exp2(m - m_new)
        m_scr[hh][...] = m_new

    def reset(hh):
        p_scr[hh][...] = jnp.zeros((tk, tq), BF16)
        acc_scr[hh][...] = jnp.zeros((FOX_ROWS, tq), F32)
        m_scr[hh][...] = jnp.full((8, tq), NEG, F32)
        al_scr[hh][...] = jnp.ones((8, tq), F32)

    def finalize(hh, qi):
        q0 = pl.multiple_of(qi * tq, tq)
        acc = acc_scr[hh][...]
        o = acc[0:FOX_DH] / acc[FOX_DH:FOX_DH + 1]
        o_ref[0, hh * FOX_DH:(hh + 1) * FOX_DH, pl.ds(q0, tq)] = o.astype(BF16)

    def finish(qi, between=None):
        for hh in range(hp):
            pv(hh, jnp.where(qi <= 1, qi, qi - 2))
        for hh in range(hp):
            softmax(hh)
        if between is not None:
            for hh in range(hp):
                between(hh)
        for hh in range(hp):
            pv(hh, jnp.maximum(qi - 1, 0))
            finalize(hh, qi)

    qTs = load_q(0)
    for hh in range(hp):
        reset(hh)
        scores(qTs, hh, 0, True)

    def q_body(qi, _):
        qTs = load_q(qi)
        finish(qi - 1, between=lambda hh: scores(qTs, hh, qi, True))
        for hh in range(hp):
            reset(hh)

        def step(u, _):
            prev_tile = jnp.where(u <= 1, qi, u - 2)
            for hh in range(hp):
                pv(hh, prev_tile)
            for hh in range(hp):
                softmax(hh)
            for hh in range(hp):
                scores(qTs, hh, u, False)
            return 0

        lax.fori_loop(0, qi, step, 0)
        return 0

    lax.fori_loop(1, nq, q_body, 0)
    finish(jnp.int32(nq - 1))


def _fox(qf, kf, vf):
    B, H, R, S = qf.shape
    tq = min(TQ, S)
    hp = FOX_HP
    specs = [pl.BlockSpec((1, 1, R, S), lambda b, h, i=i: (b, h * hp + i, 0, 0)) for i in range(hp)]
    per_head = lambda shape, dtype: [pltpu.VMEM(shape, dtype) for _ in range(hp)]
    return pl.pallas_call(
        functools.partial(_fox_kernel, tq=tq, hp=hp),
        grid=(B, H // hp),
        in_specs=specs * 3,
        out_specs=pl.BlockSpec((1, hp * FOX_DH, S), lambda b, h: (b, h, 0)),
        out_shape=jax.ShapeDtypeStruct((B, FOX_W, S), BF16),
        scratch_shapes=(
            per_head((tq, tq), F32)
            + per_head((tq, tq), BF16)
            + per_head((R, tq), F32)
            + per_head((8, tq), F32)
            + per_head((8, tq), F32)
            + per_head((8, tq), F32)
        ),
        compiler_params=pltpu.CompilerParams(
            dimension_semantics=("arbitrary", "arbitrary"),
            vmem_limit_bytes=VMEM_LIMIT),
        name="fox",
    )(*([qf] * hp + [kf] * hp + [vf] * hp))


def _merge_kernel(x_ref, or_ref, of_ref, gmix_ref, wa_ref, wro_ref, wfo_ref, wout_ref, o_ref):
    x = x_ref[0]
    ms = jnp.mean(x * x, axis=-1, keepdims=True)
    h = (x * lax.rsqrt(ms + EPS) * gmix_ref[...]).astype(BF16)
    a = jnp.dot(h, wa_ref[...], preferred_element_type=F32)
    y_r = lax.dot_general(or_ref[0], wro_ref[...], TN_DIMS, preferred_element_type=F32)
    y_f = lax.dot_general(of_ref[0], wfo_ref[...], TN_DIMS, preferred_element_type=F32)
    merged = (jax.nn.sigmoid(a[:, :D_MODEL]) * y_r
              + jax.nn.sigmoid(a[:, D_MODEL:]) * y_f).astype(BF16)
    o_ref[0] = x + jnp.dot(merged, wout_ref[...], preferred_element_type=F32)


def _merge(x, o_r, o_f, gmix, wa, wro, wfo, wout):
    B, S, D = x.shape
    tm = min(TM_MERGE, S)
    const = lambda b, s: (0, 0)
    w = lambda shape: pl.BlockSpec(shape, const, pipeline_mode=pl.Buffered(1))
    return pl.pallas_call(
        _merge_kernel,
        grid=(B, S // tm),
        in_specs=[
            pl.BlockSpec((1, tm, D), lambda b, s: (b, s, 0)),
            pl.BlockSpec((1, RET_V, tm), lambda b, s: (b, 0, s)),
            pl.BlockSpec((1, FOX_W, tm), lambda b, s: (b, 0, s)),
            pl.BlockSpec((1, D), const),
            w((D, 2 * D)), w((RET_V, D)), w((FOX_W, D)), w((D, D)),
        ],
        out_specs=pl.BlockSpec((1, tm, D), lambda b, s: (b, s, 0)),
        out_shape=jax.ShapeDtypeStruct((B, S, D), F32),
        compiler_params=pltpu.CompilerParams(
            dimension_semantics=("arbitrary", "arbitrary"),
            vmem_limit_bytes=VMEM_LIMIT),
        name="merge",
    )(x, o_r, o_f, gmix, wa, wro, wfo, wout)


def _ffn_kernel(x_ref, g_ref, wg_ref, wu_ref, wd_ref, o_ref):
    x = x_ref[0]
    ms = jnp.mean(x * x, axis=-1, keepdims=True)
    h = (x * lax.rsqrt(ms + EPS) * g_ref[...]).astype(BF16)
    gate = jnp.dot(h, wg_ref[...], preferred_element_type=F32)
    up = jnp.dot(h, wu_ref[...], preferred_element_type=F32)
    act = (gate * jax.nn.sigmoid(gate) * up).astype(BF16)
    o_ref[0] = x + jnp.dot(act, wd_ref[...], preferred_element_type=F32)


def _ffn(x, g, wg, wu, wd):
    B, S, D = x.shape
    tm = min(TM_FFN, S)
    const = lambda b, s: (0, 0)
    w = lambda shape: pl.BlockSpec(shape, const, pipeline_mode=pl.Buffered(1))
    return pl.pallas_call(
        _ffn_kernel,
        grid=(B, S // tm),
        in_specs=[
            pl.BlockSpec((1, tm, D), lambda b, s: (b, s, 0)),
            pl.BlockSpec((1, D), const),
            w((D, D_FF)), w((D, D_FF)), w((D_FF, D)),
        ],
        out_specs=pl.BlockSpec((1, tm, D), lambda b, s: (b, s, 0)),
        out_shape=jax.ShapeDtypeStruct((B, S, D), F32),
        compiler_params=pltpu.CompilerParams(
            dimension_semantics=("arbitrary", "arbitrary"),
            vmem_limit_bytes=VMEM_LIMIT),
        name="ffn",
    )(x, g, wg, wu, wd)


def _rope_tables(S):
    half = RET_DK // 2
    pos = jnp.arange(S, dtype=F32)
    inv_freq = 1.0 / (ROPE_BASE ** (jnp.arange(half, dtype=F32) / half))
    ang = pos[:, None] * inv_freq[None, :]
    return jnp.cos(ang).T, jnp.sin(ang).T


def _retention_tables():
    log_g = jnp.log1p(-(2.0 ** (-5.0 - jnp.arange(RET_HEADS, dtype=F32))))
    idx = jnp.arange(CHUNK, dtype=F32)
    diff = idx[:, None] - idx[None, :]
    decay = jnp.where(diff[None] >= 0,
                      jnp.exp(jnp.maximum(diff, 0.0)[None] * log_g[:, None, None]), 0.0)
    decay_t = jnp.swapaxes(decay, 1, 2)
    zeta = jnp.exp((CHUNK - 1.0 - idx)[None, :] * log_g[:, None])[:, None, :]
    xi = jnp.exp((idx + 1.0)[None, :] * log_g[:, None])[:, None, :]
    gch = jnp.exp(CHUNK * log_g)[:, None, None]
    return decay_t, zeta, xi, gch


def kernel(x, g_mix, w_in, b_forget, g_ret_norm, w_ret_o, g_fox_q, g_fox_k, w_fox_o,
           w_out, g_ffn, w_gate, w_up, w_down):
    B, S, D = x.shape
    depth = g_mix.shape[0]
    cos_t, sin_t = _rope_tables(S)
    decay_t, zeta, xi, gch = _retention_tables()
    n_z = Z_FF + FOX_HEADS
    for l in range(depth):
        wt = jnp.pad(w_in[l][:, :n_z].T, ((0, Z_ROWS - n_z), (0, 0))).astype(BF16)
        wa = w_in[l][:, n_z:].astype(BF16)
        gmix = g_mix[l][None, :]
        qr, kr, vr, gr, qf, kf, vf = _inproj(
            x, gmix, wt, cos_t, sin_t,
            g_fox_q[l][:, None], g_fox_k[l][:, None], b_forget[l][:, None])
        o_r = _retention(qr, kr, vr, gr, decay_t, zeta, xi, gch,
                         g_ret_norm[l].reshape(RET_HEADS, RET_DV, 1))
        o_f = _fox(qf, kf, vf)
        x = _merge(x, o_r, o_f, gmix, wa, w_ret_o[l].astype(BF16), w_fox_o[l].astype(BF16),
                   w_out[l].astype(BF16))
        x = _ffn(x, g_ffn[l][None, :], w_gate[l].astype(BF16), w_up[l].astype(BF16),
                 w_down[l].astype(BF16))
    return x
```

```python
import functools
import math

import jax
import jax.numpy as jnp
from jax import lax
from jax.experimental import pallas as pl
from jax.experimental.pallas import tpu as pltpu

F32 = jnp.float32
BF16 = jnp.bfloat16

D_MODEL = 1024
RET_HEADS = 4
RET_DK = 64
RET_DV = 128
RET_QK = RET_HEADS * RET_DK
RET_V = RET_HEADS * RET_DV
CHUNK = 128
FOX_HEADS = 8
FOX_DH = 64
FOX_W = FOX_HEADS * FOX_DH
D_FF = 2816
ROPE_BASE = 10000.0
EPS = 1e-6

Z_QR, Z_KR, Z_VR, Z_GR = 0, 256, 512, 1024
Z_QF, Z_KF, Z_VF, Z_FF = 1536, 2048, 2560, 3072
Z_ROWS = 3088
AUG = 16
FOX_ROWS = FOX_DH + AUG
NEG = -1e30

TM_IN = 1024
TS_RET = 1024
RET_UNROLL = 8
TQ = 1024
FOX_HP = 2
LOG2E = math.log2(math.e)
TM_MERGE = 1024
TM_FFN = 512
VMEM_LIMIT = 56 * 1024 * 1024

NT_DIMS = (((1,), (1,)), ((), ()))
TN_DIMS = (((0,), (0,)), ((), ()))


def _split3(c):
    hi = c.astype(BF16).astype(F32)
    r = c - hi
    mid = r.astype(BF16).astype(F32)
    lo = (r - mid).astype(BF16).astype(F32)
    return hi, mid, lo


def _inproj_kernel(x_ref, gmix_ref, wt_ref, cos_ref, sin_ref, gq_ref, gk_ref, bf_ref,
                   qr_ref, kr_ref, vr_ref, gr_ref, qf_ref, kf_ref, vf_ref,
                   carry_ref):
    tm = x_ref.shape[1]

    @pl.when(pl.program_id(1) == 0)
    def _():
        carry_ref[...] = jnp.zeros_like(carry_ref)

    x = x_ref[0]
    ms = jnp.mean(x * x, axis=-1, keepdims=True)
    h = (x * lax.rsqrt(ms + EPS) * gmix_ref[...]).astype(BF16)
    groups = ((Z_QF, Z_ROWS), (Z_VR, Z_QF), (Z_QR, Z_VR))
    parts = [(a, b, lax.dot_general(wt_ref[a:b, :], h, NT_DIMS, preferred_element_type=F32))
             for a, b in groups]

    class _Rows:
        def __getitem__(self, sl):
            for a, b, part in parts:
                if a <= sl.start and sl.stop <= b:
                    return part[sl.start - a:sl.stop - a]
            raise IndexError(sl)

    z = _Rows()

    u = z[Z_FF:Z_FF + FOX_HEADS] + bf_ref[...]
    ls = jnp.minimum(u, 0.0) - jnp.log1p(jnp.exp(-jnp.abs(u)))
    lane = lax.broadcasted_iota(jnp.int32, ls.shape, 1)
    c = ls
    d = 1
    while d < tm:
        c = c + jnp.where(lane >= d, pltpu.roll(c, d, axis=1), 0.0)
        d *= 2
    c = c + carry_ref[:, 0:1]
    carry_ref[...] = jnp.broadcast_to(c[:, tm - 1:tm], carry_ref.shape)

    ridx = lax.broadcasted_iota(jnp.int32, (AUG, tm), 0)
    scale = FOX_DH ** -0.5 * LOG2E
    c2 = c * LOG2E
    for hh in range(FOX_HEADS):
        r0 = hh * FOX_DH
        q = z[Z_QF + r0:Z_QF + r0 + FOX_DH]
        k = z[Z_KF + r0:Z_KF + r0 + FOX_DH]
        qn = q * lax.rsqrt(jnp.mean(q * q, axis=0, keepdims=True) + EPS) * gq_ref[...]
        kn = k * lax.rsqrt(jnp.mean(k * k, axis=0, keepdims=True) + EPS) * gk_ref[...]
        c_hi, c_mid, c_lo = _split3(c2[hh:hh + 1, :])
        q_aug = jnp.where(ridx == 0, c_hi, jnp.where(ridx == 1, c_mid,
                          jnp.where(ridx == 2, c_lo, jnp.where(ridx < 6, 1.0, 0.0))))
        k_aug = jnp.where(ridx == 3, -c_hi, jnp.where(ridx == 4, -c_mid,
                          jnp.where(ridx == 5, -c_lo, jnp.where(ridx < 3, 1.0, 0.0))))
        qf_ref[0, hh, 0:FOX_DH, :] = (qn * scale).astype(BF16)
        qf_ref[0, hh, FOX_DH:FOX_ROWS, :] = q_aug.astype(BF16)
        kf_ref[0, hh, 0:FOX_DH, :] = kn.astype(BF16)
        kf_ref[0, hh, FOX_DH:FOX_ROWS, :] = k_aug.astype(BF16)
        vf_ref[0, hh, 0:FOX_DH, :] = z[Z_VF + r0:Z_VF + r0 + FOX_DH].astype(BF16)
        vf_ref[0, hh, FOX_DH:FOX_ROWS, :] = jnp.where(ridx == 0, 1.0, 0.0).astype(BF16)

    vr_ref[0] = z[Z_VR:Z_VR + RET_V].astype(BF16)
    g = z[Z_GR:Z_GR + RET_V]
    gr_ref[0] = (g * jax.nn.sigmoid(g)).astype(BF16)

    cos = cos_ref[...]
    sin = sin_ref[...]
    half = RET_DK // 2
    for hh in range(RET_HEADS):
        for base, ref, scl in ((Z_QR, qr_ref, 1.0), (Z_KR, kr_ref, RET_DK ** -0.5)):
            r0 = base + hh * RET_DK
            x1 = z[r0:r0 + half]
            x2 = z[r0 + half:r0 + RET_DK]
            o1 = (x1 * cos - x2 * sin) * scl
            o2 = (x1 * sin + x2 * cos) * scl
            ref[0, hh * RET_DK:hh * RET_DK + half, :] = o1.astype(BF16)
            ref[0, hh * RET_DK + half:(hh + 1) * RET_DK, :] = o2.astype(BF16)


def _inproj(x, gmix, wt, cos_t, sin_t, gq, gk, bf):
    B, S, D = x.shape
    tm = min(TM_IN, S)
    grid = (B, S // tm)
    const = lambda b, s: (0, 0)
    fm = lambda rows: pl.BlockSpec((1, rows, tm), lambda b, s: (b, 0, s))
    fox = pl.BlockSpec((1, FOX_HEADS, FOX_ROWS, tm), lambda b, s: (b, 0, 0, s))
    out_shape = (
        jax.ShapeDtypeStruct((B, RET_QK, S), BF16),
        jax.ShapeDtypeStruct((B, RET_QK, S), BF16),
        jax.ShapeDtypeStruct((B, RET_V, S), BF16),
        jax.ShapeDtypeStruct((B, RET_V, S), BF16),
        jax.ShapeDtypeStruct((B, FOX_HEADS, FOX_ROWS, S), BF16),
        jax.ShapeDtypeStruct((B, FOX_HEADS, FOX_ROWS, S), BF16),
        jax.ShapeDtypeStruct((B, FOX_HEADS, FOX_ROWS, S), BF16),
    )
    return pl.pallas_call(
        _inproj_kernel,
        grid=grid,
        in_specs=[
            pl.BlockSpec((1, tm, D), lambda b, s: (b, s, 0)),
            pl.BlockSpec((1, D), const),
            pl.BlockSpec((Z_ROWS, D), const, pipeline_mode=pl.Buffered(1)),
            pl.BlockSpec((RET_DK // 2, tm), lambda b, s: (0, s)),
            pl.BlockSpec((RET_DK // 2, tm), lambda b, s: (0, s)),
            pl.BlockSpec((FOX_DH, 1), const),
            pl.BlockSpec((FOX_DH, 1), const),
            pl.BlockSpec((FOX_HEADS, 1), const),
        ],
        out_specs=(fm(RET_QK), fm(RET_QK), fm(RET_V), fm(RET_V), fox, fox, fox),
        out_shape=out_shape,
        scratch_shapes=[pltpu.VMEM((FOX_HEADS, 128), F32)],
        compiler_params=pltpu.CompilerParams(
            dimension_semantics=("arbitrary", "arbitrary"),
            vmem_limit_bytes=VMEM_LIMIT),
        name="inproj",
    )(x, gmix, wt, cos_t, sin_t, gq, gk, bf)


def _retention_kernel(q_ref, k_ref, v_ref, g_ref, decay_ref, zeta_ref, xi_ref, gch_ref, gn_ref,
                      o_ref, st_ref):
    ts = q_ref.shape[2]

    @pl.when(pl.program_id(1) == 0)
    def _():
        st_ref[...] = jnp.zeros_like(st_ref)

    def chunk_body(ci, _):
        t0 = pl.multiple_of(ci * CHUNK, CHUNK)
        heads = range(RET_HEADS)
        qT = [q_ref[0, hh * RET_DK:(hh + 1) * RET_DK, pl.ds(t0, CHUNK)] for hh in heads]
        kT = [k_ref[0, hh * RET_DK:(hh + 1) * RET_DK, pl.ds(t0, CHUNK)] for hh in heads]
        vT = [v_ref[0, hh * RET_DV:(hh + 1) * RET_DV, pl.ds(t0, CHUNK)] for hh in heads]
        st = [st_ref[hh] for hh in heads]
        sT = [lax.dot_general(kT[hh], qT[hh], TN_DIMS, preferred_element_type=F32) for hh in heads]
        inter = [lax.dot_general(st[hh].astype(BF16), qT[hh], TN_DIMS,
                                 preferred_element_type=F32) for hh in heads]
        for hh in heads:
            vz = (vT[hh].astype(F32) * zeta_ref[hh]).astype(BF16)
            kv = lax.dot_general(kT[hh], vz, NT_DIMS, preferred_element_type=F32)
            st_ref[hh] = gch_ref[hh] * st[hh] + kv
        for hh in heads:
            pT = (sT[hh] * decay_ref[hh]).astype(BF16)
            intra = jnp.dot(vT[hh], pT, preferred_element_type=F32)
            o = intra + inter[hh] * xi_ref[hh]
            mu = jnp.mean(o, axis=0, keepdims=True)
            oc = o - mu
            var = jnp.mean(oc * oc, axis=0, keepdims=True)
            on = oc * lax.rsqrt(var + EPS) * gn_ref[hh]
            sg = g_ref[0, hh * RET_DV:(hh + 1) * RET_DV, pl.ds(t0, CHUNK)]
            o_ref[0, hh * RET_DV:(hh + 1) * RET_DV, pl.ds(t0, CHUNK)] = (
                sg.astype(F32) * on).astype(BF16)
        return 0

    lax.fori_loop(0, ts // CHUNK, chunk_body, 0, unroll=RET_UNROLL)


def _retention(qr, kr, vr, gr, decay_t, zeta, xi, gch, gn):
    B, _, S = qr.shape
    ts = min(TS_RET, S)
    grid = (B, S // ts)
    fm = lambda rows: pl.BlockSpec((1, rows, ts), lambda b, s: (b, 0, s))
    c3 = lambda b, s: (0, 0, 0)
    return pl.pallas_call(
        _retention_kernel,
        grid=grid,
        in_specs=[
            fm(RET_QK), fm(RET_QK), fm(RET_V), fm(RET_V),
            pl.BlockSpec((RET_HEADS, CHUNK, CHUNK), c3),
            pl.BlockSpec((RET_HEADS, 1, CHUNK), c3),
            pl.BlockSpec((RET_HEADS, 1, CHUNK), c3),
            pl.BlockSpec((RET_HEADS, 1, 1), c3),
            pl.BlockSpec((RET_HEADS, RET_DV, 1), c3),
        ],
        out_specs=fm(RET_V),
        out_shape=jax.ShapeDtypeStruct((B, RET_V, S), BF16),
        scratch_shapes=[pltpu.VMEM((RET_HEADS, RET_DK, RET_DV), F32)],
        compiler_params=pltpu.CompilerParams(
            dimension_semantics=("arbitrary", "arbitrary"),
            vmem_limit_bytes=VMEM_LIMIT),
        name="retention",
    )(qr, kr, vr, gr, decay_t, zeta, xi, gch, gn)


def _fox_kernel(*refs, tq, hp):
    q_refs, k_refs, v_refs = refs[0:hp], refs[hp:2 * hp], refs[2 * hp:3 * hp]
    o_ref = refs[3 * hp]
    s_scr, p_scr, acc_scr, m_scr, al_scr, mx_scr = (
        refs[3 * hp + 1 + i * hp:3 * hp + 1 + (i + 1) * hp] for i in range(6))
    S = q_refs[0].shape[3]
    tk = tq
    nq = S // tq

    def load_q(qi):
        q0 = pl.multiple_of(qi * tq, tq)
        return [q_refs[hh][0, 0, :, pl.ds(q0, tq)] for hh in range(hp)]

    def scores(qTs, hh, tile, masked):
        k0 = pl.multiple_of(tile * tk, tk)
        kT = k_refs[hh][0, 0, :, pl.ds(k0, tk)]
        s = lax.dot_general(kT, qTs[hh], TN_DIMS, preferred_element_type=F32)
        if masked:
            row = lax.broadcasted_iota(jnp.int32, (tk, tq), 0)
            col = lax.broadcasted_iota(jnp.int32, (tk, tq), 1)
            s = jnp.where(row > col, NEG, s)
        s_scr[hh][...] = s
        mx_scr[hh][...] = jnp.broadcast_to(jnp.max(s, axis=0, keepdims=True), (8, tq))

    def pv(hh, tile):
        k0 = pl.multiple_of(tile * tk, tk)
        vT = v_refs[hh][0, 0, :, pl.ds(k0, tk)]
        acc_scr[hh][...] = (acc_scr[hh][...] * al_scr[hh][...][0:1]
                       + jnp.dot(vT, p_scr[hh][...], preferred_element_type=F32))

    def softmax(hh):
        m = m_scr[hh][...]
        m_new = jnp.maximum(m, mx_scr[hh][...])
        p_scr[hh][...] = jnp.exp2(s_scr[hh][...] - m_new[0:1]).astype(BF16)
        al_scr[hh][...] = jnp.exp2(m - m_new)
        m_scr[hh][...] = m_new

    def reset(hh):
        p_scr[hh][...] = jnp.zeros((tk, tq), BF16)
        acc_scr[hh][...] = jnp.zeros((FOX_ROWS, tq), F32)
        m_scr[hh][...] = jnp.full((8, tq), NEG, F32)
        al_scr[hh][...] = jnp.ones((8, tq), F32)

    def finalize(hh, qi):
        q0 = pl.multiple_of(qi * tq, tq)
        acc = acc_scr[hh][...]
        o = acc[0:FOX_DH] / acc[FOX_DH:FOX_DH + 1]
        o_ref[0, hh * FOX_DH:(hh + 1) * FOX_DH, pl.ds(q0, tq)] = o.astype(BF16)

    def finish(qi, between=None):
        for hh in range(hp):
            pv(hh, jnp.where(qi <= 1, qi, qi - 2))
        for hh in range(hp):
            softmax(hh)
        if between is not None:
            for hh in range(hp):
                between(hh)
        for hh in range(hp):
            pv(hh, jnp.maximum(qi - 1, 0))
            finalize(hh, qi)

    qTs = load_q(0)
    for hh in range(hp):
        reset(hh)
        scores(qTs, hh, 0, True)

    def q_body(qi, _):
        qTs = load_q(qi)
        finish(qi - 1, between=lambda hh: scores(qTs, hh, qi, True))
        for hh in range(hp):
            reset(hh)

        def step(u, _):
            prev_tile = jnp.where(u <= 1, qi, u - 2)
            for hh in range(hp):
                pv(hh, prev_tile)
            for hh in range(hp):
                softmax(hh)
            for hh in range(hp):
                scores(qTs, hh, u, False)
            return 0

        lax.fori_loop(0, qi, step, 0)
        return 0

    lax.fori_loop(1, nq, q_body, 0)
    finish(jnp.int32(nq - 1))


def _fox(qf, kf, vf):
    B, H, R, S = qf.shape
    tq = min(TQ, S)
    hp = FOX_HP
    specs = [pl.BlockSpec((1, 1, R, S), lambda b, h, i=i: (b, h * hp + i, 0, 0)) for i in range(hp)]
    per_head = lambda shape, dtype: [pltpu.VMEM(shape, dtype) for _ in range(hp)]
    return pl.pallas_call(
        functools.partial(_fox_kernel, tq=tq, hp=hp),
        grid=(B, H // hp),
        in_specs=specs * 3,
        out_specs=pl.BlockSpec((1, hp * FOX_DH, S), lambda b, h: (b, h, 0)),
        out_shape=jax.ShapeDtypeStruct((B, FOX_W, S), BF16),
        scratch_shapes=(
            per_head((tq, tq), F32)
            + per_head((tq, tq), BF16)
            + per_head((R, tq), F32)
            + per_head((8, tq), F32)
            + per_head((8, tq), F32)
            + per_head((8, tq), F32)
        ),
        compiler_params=pltpu.CompilerParams(
            dimension_semantics=("arbitrary", "arbitrary"),
            vmem_limit_bytes=VMEM_LIMIT),
        name="fox",
    )(*([qf] * hp + [kf] * hp + [vf] * hp))


def _merge_kernel(x_ref, or_ref, of_ref, gmix_ref, wa_ref, wro_ref, wfo_ref, wout_ref, o_ref):
    x = x_ref[0]
    ms = jnp.mean(x * x, axis=-1, keepdims=True)
    h = (x * lax.rsqrt(ms + EPS) * gmix_ref[...]).astype(BF16)
    a = jnp.dot(h, wa_ref[...], preferred_element_type=F32)
    y_r = lax.dot_general(or_ref[0], wro_ref[...], TN_DIMS, preferred_element_type=F32)
    y_f = lax.dot_general(of_ref[0], wfo_ref[...], TN_DIMS, preferred_element_type=F32)
    merged = (jax.nn.sigmoid(a[:, :D_MODEL]) * y_r
              + jax.nn.sigmoid(a[:, D_MODEL:]) * y_f).astype(BF16)
    o_ref[0] = x + jnp.dot(merged, wout_ref[...], preferred_element_type=F32)


def _merge(x, o_r, o_f, gmix, wa, wro, wfo, wout):
    B, S, D = x.shape
    tm = min(TM_MERGE, S)
    const = lambda b, s: (0, 0)
    w = lambda shape: pl.BlockSpec(shape, const, pipeline_mode=pl.Buffered(1))
    return pl.pallas_call(
        _merge_kernel,
        grid=(B, S // tm),
        in_specs=[
            pl.BlockSpec((1, tm, D), lambda b, s: (b, s, 0)),
            pl.BlockSpec((1, RET_V, tm), lambda b, s: (b, 0, s)),
            pl.BlockSpec((1, FOX_W, tm), lambda b, s: (b, 0, s)),
            pl.BlockSpec((1, D), const),
            w((D, 2 * D)), w((RET_V, D)), w((FOX_W, D)), w((D, D)),
        ],
        out_specs=pl.BlockSpec((1, tm, D), lambda b, s: (b, s, 0)),
        out_shape=jax.ShapeDtypeStruct((B, S, D), F32),
        compiler_params=pltpu.CompilerParams(
            dimension_semantics=("arbitrary", "arbitrary"),
            vmem_limit_bytes=VMEM_LIMIT),
        name="merge",
    )(x, o_r, o_f, gmix, wa, wro, wfo, wout)


def _ffn_kernel(x_ref, g_ref, wg_ref, wu_ref, wd_ref, o_ref):
    x = x_ref[0]
    ms = jnp.mean(x * x, axis=-1, keepdims=True)
    h = (x * lax.rsqrt(ms + EPS) * g_ref[...]).astype(BF16)
    gate = jnp.dot(h, wg_ref[...], preferred_element_type=F32)
    up = jnp.dot(h, wu_ref[...], preferred_element_type=F32)
    act = (gate * jax.nn.sigmoid(gate) * up).astype(BF16)
    o_ref[0] = x + jnp.dot(act, wd_ref[...], preferred_element_type=F32)


def _ffn(x, g, wg, wu, wd):
    B, S, D = x.shape
    tm = min(TM_FFN, S)
    const = lambda b, s: (0, 0)
    w = lambda shape: pl.BlockSpec(shape, const, pipeline_mode=pl.Buffered(1))
    return pl.pallas_call(
        _ffn_kernel,
        grid=(B, S // tm),
        in_specs=[
            pl.BlockSpec((1, tm, D), lambda b, s: (b, s, 0)),
            pl.BlockSpec((1, D), const),
            w((D, D_FF)), w((D, D_FF)), w((D_FF, D)),
        ],
        out_specs=pl.BlockSpec((1, tm, D), lambda b, s: (b, s, 0)),
        out_shape=jax.ShapeDtypeStruct((B, S, D), F32),
        compiler_params=pltpu.CompilerParams(
            dimension_semantics=("arbitrary", "arbitrary"),
            vmem_limit_bytes=VMEM_LIMIT),
        name="ffn",
    )(x, g, wg, wu, wd)


def _rope_tables(S):
    half = RET_DK // 2
    pos = jnp.arange(S, dtype=F32)
    inv_freq = 1.0 / (ROPE_BASE ** (jnp.arange(half, dtype=F32) / half))
    ang = pos[:, None] * inv_freq[None, :]
    return jnp.cos(ang).T, jnp.sin(ang).T


def _retention_tables():
    log_g = jnp.log1p(-(2.0 ** (-5.0 - jnp.arange(RET_HEADS, dtype=F32))))
    idx = jnp.arange(CHUNK, dtype=F32)
    diff = idx[:, None] - idx[None, :]
    decay = jnp.where(diff[None] >= 0,
                      jnp.exp(jnp.maximum(diff, 0.0)[None] * log_g[:, None, None]), 0.0)
    decay_t = jnp.swapaxes(decay, 1, 2)
    zeta = jnp.exp((CHUNK - 1.0 - idx)[None, :] * log_g[:, None])[:, None, :]
    xi = jnp.exp((idx + 1.0)[None, :] * log_g[:, None])[:, None, :]
    gch = jnp.exp(CHUNK * log_g)[:, None, None]
    return decay_t, zeta, xi, gch


def kernel(x, g_mix, w_in, b_forget, g_ret_norm, w_ret_o, g_fox_q, g_fox_k, w_fox_o,
           w_out, g_ffn, w_gate, w_up, w_down):
    B, S, D = x.shape
    depth = g_mix.shape[0]
    cos_t, sin_t = _rope_tables(S)
    decay_t, zeta, xi, gch = _retention_tables()
    n_z = Z_FF + FOX_HEADS
    for l in range(depth):
        wt = jnp.pad(w_in[l][:, :n_z].T, ((0, Z_ROWS - n_z), (0, 0))).astype(BF16)
        wa = w_in[l][:, n_z:].astype(BF16)
        gmix = g_mix[l][None, :]
        qr, kr, vr, gr, qf, kf, vf = _inproj(
            x, gmix, wt, cos_t, sin_t,
            g_fox_q[l][:, None], g_fox_k[l][:, None], b_forget[l][:, None])
        o_r = _retention(qr, kr, vr, gr, decay_t, zeta, xi, gch,
                         g_ret_norm[l].reshape(RET_HEADS, RET_DV, 1))
        o_f = _fox(qf, kf, vf)
        x = _merge(x, o_r, o_f, gmix, wa, w_ret_o[l].astype(BF16), w_fox_o[l].astype(BF16),
                   w_out[l].astype(BF16))
        x = _ffn(x, g_ffn[l][None, :], w_gate[l].astype(BF16), w_up[l].astype(BF16),
                 w_down[l].astype(BF16))
    return x
```

```python
import functools
import math

import jax
import jax.numpy as jnp
from jax import lax
from jax.experimental import pallas as pl
from jax.experimental.pallas import tpu as pltpu

F32 = jnp.float32
BF16 = jnp.bfloat16

D_MODEL = 1024
RET_HEADS = 4
RET_DK = 64
RET_DV = 128
RET_QK = RET_HEADS * RET_DK
RET_V = RET_HEADS * RET_DV
CHUNK = 128
FOX_HEADS = 8
FOX_DH = 64
FOX_W = FOX_HEADS * FOX_DH
D_FF = 2816
ROPE_BASE = 10000.0
EPS = 1e-6

Z_QR, Z_KR, Z_VR, Z_GR = 0, 256, 512, 1024
Z_QF, Z_KF, Z_VF, Z_FF = 1536, 2048, 2560, 3072
Z_ROWS = 3088
AUG = 16
FOX_ROWS = FOX_DH + AUG
NEG = -1e30

TM_IN = 1024
TS_RET = 1024
RET_UNROLL = 8
TQ = 1024
FOX_SUB = 256
FOX_HP = 2
LOG2E = math.log2(math.e)
TM_MERGE = 1024
TM_FFN = 512
VMEM_LIMIT = 56 * 1024 * 1024

NT_DIMS = (((1,), (1,)), ((), ()))
TN_DIMS = (((0,), (0,)), ((), ()))


def _split3(c):
    hi = c.astype(BF16).astype(F32)
    r = c - hi
    mid = r.astype(BF16).astype(F32)
    lo = (r - mid).astype(BF16).astype(F32)
    return hi, mid, lo


def _inproj_kernel(x_ref, gmix_ref, wt_ref, cos_ref, sin_ref, gq_ref, gk_ref, bf_ref,
                   qr_ref, kr_ref, vr_ref, gr_ref, qf_ref, kf_ref, vf_ref,
                   carry_ref):
    tm = x_ref.shape[1]

    @pl.when(pl.program_id(1) == 0)
    def _():
        carry_ref[...] = jnp.zeros_like(carry_ref)

    x = x_ref[0]
    ms = jnp.mean(x * x, axis=-1, keepdims=True)
    h = (x * lax.rsqrt(ms + EPS) * gmix_ref[...]).astype(BF16)
    groups = ((Z_QF, Z_ROWS), (Z_VR, Z_QF), (Z_QR, Z_VR))
    parts = [(a, b, lax.dot_general(wt_ref[a:b, :], h, NT_DIMS, preferred_element_type=F32))
             for a, b in groups]

    class _Rows:
        def __getitem__(self, sl):
            for a, b, part in parts:
                if a <= sl.start and sl.stop <= b:
                    return part[sl.start - a:sl.stop - a]
            raise IndexError(sl)

    z = _Rows()

    u = z[Z_FF:Z_FF + FOX_HEADS] + bf_ref[...]
    ls = jnp.minimum(u, 0.0) - jnp.log1p(jnp.exp(-jnp.abs(u)))
    lane = lax.broadcasted_iota(jnp.int32, ls.shape, 1)
    c = ls
    d = 1
    while d < tm:
        c = c + jnp.where(lane >= d, pltpu.roll(c, d, axis=1), 0.0)
        d *= 2
    c = c + carry_ref[:, 0:1]
    carry_ref[...] = jnp.broadcast_to(c[:, tm - 1:tm], carry_ref.shape)

    ridx = lax.broadcasted_iota(jnp.int32, (AUG, tm), 0)
    scale = FOX_DH ** -0.5 * LOG2E
    c2 = c * LOG2E
    for hh in range(FOX_HEADS):
        r0 = hh * FOX_DH
        q = z[Z_QF + r0:Z_QF + r0 + FOX_DH]
        k = z[Z_KF + r0:Z_KF + r0 + FOX_DH]
        qn = q * lax.rsqrt(jnp.mean(q * q, axis=0, keepdims=True) + EPS) * gq_ref[...]
        kn = k * lax.rsqrt(jnp.mean(k * k, axis=0, keepdims=True) + EPS) * gk_ref[...]
        c_hi, c_mid, c_lo = _split3(c2[hh:hh + 1, :])
        q_aug = jnp.where(ridx == 0, c_hi, jnp.where(ridx == 1, c_mid,
                          jnp.where(ridx == 2, c_lo, jnp.where(ridx < 6, 1.0, 0.0))))
        k_aug = jnp.where(ridx == 3, -c_hi, jnp.where(ridx == 4, -c_mid,
                          jnp.where(ridx == 5, -c_lo, jnp.where(ridx < 3, 1.0, 0.0))))
        qf_ref[0, hh, 0:FOX_DH, :] = (qn * scale).astype(BF16)
        qf_ref[0, hh, FOX_DH:FOX_ROWS, :] = q_aug.astype(BF16)
        kf_ref[0, hh, 0:FOX_DH, :] = kn.astype(BF16)
        kf_ref[0, hh, FOX_DH:FOX_ROWS, :] = k_aug.astype(BF16)
        vf_ref[0, hh, 0:FOX_DH, :] = z[Z_VF + r0:Z_VF + r0 + FOX_DH].astype(BF16)
        vf_ref[0, hh, FOX_DH:FOX_ROWS, :] = jnp.where(ridx == 0, 1.0, 0.0).astype(BF16)

    vr_ref[0] = z[Z_VR:Z_VR + RET_V].astype(BF16)
    g = z[Z_GR:Z_GR + RET_V]
    gr_ref[0] = (g * jax.nn.sigmoid(g)).astype(BF16)

    cos = cos_ref[...]
    sin = sin_ref[...]
    half = RET_DK // 2
    for hh in range(RET_HEADS):
        for base, ref, scl in ((Z_QR, qr_ref, 1.0), (Z_KR, kr_ref, RET_DK ** -0.5)):
            r0 = base + hh * RET_DK
            x1 = z[r0:r0 + half]
            x2 = z[r0 + half:r0 + RET_DK]
            o1 = (x1 * cos - x2 * sin) * scl
            o2 = (x1 * sin + x2 * cos) * scl
            ref[0, hh * RET_DK:hh * RET_DK + half, :] = o1.astype(BF16)
            ref[0, hh * RET_DK + half:(hh + 1) * RET_DK, :] = o2.astype(BF16)


def _inproj(x, gmix, wt, cos_t, sin_t, gq, gk, bf):
    B, S, D = x.shape
    tm = min(TM_IN, S)
    grid = (B, S // tm)
    const = lambda b, s: (0, 0)
    fm = lambda rows: pl.BlockSpec((1, rows, tm), lambda b, s: (b, 0, s))
    fox = pl.BlockSpec((1, FOX_HEADS, FOX_ROWS, tm), lambda b, s: (b, 0, 0, s))
    out_shape = (
        jax.ShapeDtypeStruct((B, RET_QK, S), BF16),
        jax.ShapeDtypeStruct((B, RET_QK, S), BF16),
        jax.ShapeDtypeStruct((B, RET_V, S), BF16),
        jax.ShapeDtypeStruct((B, RET_V, S), BF16),
        jax.ShapeDtypeStruct((B, FOX_HEADS, FOX_ROWS, S), BF16),
        jax.ShapeDtypeStruct((B, FOX_HEADS, FOX_ROWS, S), BF16),
        jax.ShapeDtypeStruct((B, FOX_HEADS, FOX_ROWS, S), BF16),
    )
    return pl.pallas_call(
        _inproj_kernel,
        grid=grid,
        in_specs=[
            pl.BlockSpec((1, tm, D), lambda b, s: (b, s, 0)),
            pl.BlockSpec((1, D), const),
            pl.BlockSpec((Z_ROWS, D), const, pipeline_mode=pl.Buffered(1)),
            pl.BlockSpec((RET_DK // 2, tm), lambda b, s: (0, s)),
            pl.BlockSpec((RET_DK // 2, tm), lambda b, s: (0, s)),
            pl.BlockSpec((FOX_DH, 1), const),
            pl.BlockSpec((FOX_DH, 1), const),
            pl.BlockSpec((FOX_HEADS, 1), const),
        ],
        out_specs=(fm(RET_QK), fm(RET_QK), fm(RET_V), fm(RET_V), fox, fox, fox),
        out_shape=out_shape,
        scratch_shapes=[pltpu.VMEM((FOX_HEADS, 128), F32)],
        compiler_params=pltpu.CompilerParams(
            dimension_semantics=("arbitrary", "arbitrary"),
            vmem_limit_bytes=VMEM_LIMIT),
        name="inproj",
    )(x, gmix, wt, cos_t, sin_t, gq, gk, bf)


def _retention_kernel(q_ref, k_ref, v_ref, g_ref, decay_ref, zeta_ref, xi_ref, gch_ref, gn_ref,
                      o_ref, st_ref):
    ts = q_ref.shape[2]

    @pl.when(pl.program_id(1) == 0)
    def _():
        st_ref[...] = jnp.zeros_like(st_ref)

    def chunk_body(ci, _):
        t0 = pl.multiple_of(ci * CHUNK, CHUNK)
        heads = range(RET_HEADS)
        qT = [q_ref[0, hh * RET_DK:(hh + 1) * RET_DK, pl.ds(t0, CHUNK)] for hh in heads]
        kT = [k_ref[0, hh * RET_DK:(hh + 1) * RET_DK, pl.ds(t0, CHUNK)] for hh in heads]
        vT = [v_ref[0, hh * RET_DV:(hh + 1) * RET_DV, pl.ds(t0, CHUNK)] for hh in heads]
        st = [st_ref[hh] for hh in heads]
        sT = [lax.dot_general(kT[hh], qT[hh], TN_DIMS, preferred_element_type=F32) for hh in heads]
        inter = [lax.dot_general(st[hh].astype(BF16), qT[hh], TN_DIMS,
                                 preferred_element_type=F32) for hh in heads]
        for hh in heads:
            vz = (vT[hh].astype(F32) * zeta_ref[hh]).astype(BF16)
            kv = lax.dot_general(kT[hh], vz, NT_DIMS, preferred_element_type=F32)
            st_ref[hh] = gch_ref[hh] * st[hh] + kv
        for hh in heads:
            pT = (sT[hh] * decay_ref[hh]).astype(BF16)
            intra = jnp.dot(vT[hh], pT, preferred_element_type=F32)
            o = intra + inter[hh] * xi_ref[hh]
            mu = jnp.mean(o, axis=0, keepdims=True)
            oc = o - mu
            var = jnp.mean(oc * oc, axis=0, keepdims=True)
            on = oc * lax.rsqrt(var + EPS) * gn_ref[hh]
            sg = g_ref[0, hh * RET_DV:(hh + 1) * RET_DV, pl.ds(t0, CHUNK)]
            o_ref[0, hh * RET_DV:(hh + 1) * RET_DV, pl.ds(t0, CHUNK)] = (
                sg.astype(F32) * on).astype(BF16)
        return 0

    lax.fori_loop(0, ts // CHUNK, chunk_body, 0, unroll=RET_UNROLL)


def _retention(qr, kr, vr, gr, decay_t, zeta, xi, gch, gn):
    B, _, S = qr.shape
    ts = min(TS_RET, S)
    grid = (B, S // ts)
    fm = lambda rows: pl.BlockSpec((1, rows, ts), lambda b, s: (b, 0, s))
    c3 = lambda b, s: (0, 0, 0)
    return pl.pallas_call(
        _retention_kernel,
        grid=grid,
        in_specs=[
            fm(RET_QK), fm(RET_QK), fm(RET_V), fm(RET_V),
            pl.BlockSpec((RET_HEADS, CHUNK, CHUNK), c3),
            pl.BlockSpec((RET_HEADS, 1, CHUNK), c3),
            pl.BlockSpec((RET_HEADS, 1, CHUNK), c3),
            pl.BlockSpec((RET_HEADS, 1, 1), c3),
            pl.BlockSpec((RET_HEADS, RET_DV, 1), c3),
        ],
        out_specs=fm(RET_V),
        out_shape=jax.ShapeDtypeStruct((B, RET_V, S), BF16),
        scratch_shapes=[pltpu.VMEM((RET_HEADS, RET_DK, RET_DV), F32)],
        compiler_params=pltpu.CompilerParams(
            dimension_semantics=("arbitrary", "arbitrary"),
            vmem_limit_bytes=VMEM_LIMIT),
        name="retention",
    )(qr, kr, vr, gr, decay_t, zeta, xi, gch, gn)


def _fox_kernel(*refs, tq, hp):
    q_refs, k_refs, v_refs = refs[0:hp], refs[hp:2 * hp], refs[2 * hp:3 * hp]
    o_ref = refs[3 * hp]
    s_scr, p_scr, acc_scr, m_scr, al_scr, mx_scr = (
        refs[3 * hp + 1 + i * hp:3 * hp + 1 + (i + 1) * hp] for i in range(6))
    S = q_refs[0].shape[3]
    tk = tq
    nq = S // tq

    def load_q(qi):
        q0 = pl.multiple_of(qi * tq, tq)
        return [q_refs[hh][0, 0, :, pl.ds(q0, tq)] for hh in range(hp)]

    def scores(qTs, hh, tile, masked):
        k0 = pl.multiple_of(tile * tk, tk)
        kT = k_refs[hh][0, 0, :, pl.ds(k0, tk)]
        if masked:
            sb = min(FOX_SUB, tq)
            tri = (lax.broadcasted_iota(jnp.int32, (sb, sb), 0)
                   > lax.broadcasted_iota(jnp.int32, (sb, sb), 1))
            for c in range(tq // sb):
                lo, hi = c * sb, (c + 1) * sb
                sc = lax.dot_general(kT[:, :hi], qTs[hh][:, lo:hi], TN_DIMS,
                                     preferred_element_type=F32)
                diag = jnp.where(tri, NEG, sc[lo:hi])
                s_scr[hh][lo:hi, lo:hi] = diag
                mx = jnp.max(diag, axis=0, keepdims=True)
                if c > 0:
                    s_scr[hh][0:lo, lo:hi] = sc[0:lo]
                    mx = jnp.maximum(mx, jnp.max(sc[0:lo], axis=0, keepdims=True))
                if hi < tk:
                    s_scr[hh][hi:tk, lo:hi] = jnp.full((tk - hi, sb), NEG, F32)
                mx_scr[hh][:, lo:hi] = jnp.broadcast_to(mx, (8, sb))
            return
        s = lax.dot_general(kT, qTs[hh], TN_DIMS, preferred_element_type=F32)
        s_scr[hh][...] = s
        mx_scr[hh][...] = jnp.broadcast_to(jnp.max(s, axis=0, keepdims=True), (8, tq))

    def pv(hh, tile):
        k0 = pl.multiple_of(tile * tk, tk)
        vT = v_refs[hh][0, 0, :, pl.ds(k0, tk)]
        acc_scr[hh][...] = (acc_scr[hh][...] * al_scr[hh][...][0:1]
                       + jnp.dot(vT, p_scr[hh][...], preferred_element_type=F32))

    def softmax(hh):
        m = m_scr[hh][...]
        m_new = jnp.maximum(m, mx_scr[hh][...])
        p_scr[hh][...] = jnp.exp2(s_scr[hh][...] - m_new[0:1]).astype(BF16)
        al_scr[hh][...] = jnp.exp2(m - m_new)
        m_scr[hh][...] = m_new

    def reset(hh):
        p_scr[hh][...] = jnp.zeros((tk, tq), BF16)
        acc_scr[hh][...] = jnp.zeros((FOX_ROWS, tq), F32)
        m_scr[hh][...] = jnp.full((8, tq), NEG, F32)
        al_scr[hh][...] = jnp.ones((8, tq), F32)

    def finalize(hh, qi):
        q0 = pl.multiple_of(qi * tq, tq)
        acc = acc_scr[hh][...]
        o = acc[0:FOX_DH] / acc[FOX_DH:FOX_DH + 1]
        o_ref[0, hh * FOX_DH:(hh + 1) * FOX_DH, pl.ds(q0, tq)] = o.astype(BF16)

    def finish(qi, between=None):
        for hh in range(hp):
            pv(hh, jnp.where(qi <= 1, qi, qi - 2))
        for hh in range(hp):
            softmax(hh)
        if between is not None:
            for hh in range(hp):
                between(hh)
        for hh in range(hp):
            pv(hh, jnp.maximum(qi - 1, 0))
            finalize(hh, qi)

    qTs = load_q(0)
    for hh in range(hp):
        reset(hh)
        scores(qTs, hh, 0, True)

    def q_body(qi, _):
        qTs = load_q(qi)
        finish(qi - 1, between=lambda hh: scores(qTs, hh, qi, True))
        for hh in range(hp):
            reset(hh)

        def step(u, _):
            prev_tile = jnp.where(u <= 1, qi, u - 2)
            for hh in range(hp):
                pv(hh, prev_tile)
            for hh in range(hp):
                softmax(hh)
            for hh in range(hp):
                scores(qTs, hh, u, False)
            return 0

        lax.fori_loop(0, qi, step, 0)
        return 0

    lax.fori_loop(1, nq, q_body, 0)
    finish(jnp.int32(nq - 1))


def _fox(qf, kf, vf):
    B, H, R, S = qf.shape
    tq = min(TQ, S)
    hp = FOX_HP
    specs = [pl.BlockSpec((1, 1, R, S), lambda b, h, i=i: (b, h * hp + i, 0, 0)) for i in range(hp)]
    per_head = lambda shape, dtype: [pltpu.VMEM(shape, dtype) for _ in range(hp)]
    return pl.pallas_call(
        functools.partial(_fox_kernel, tq=tq, hp=hp),
        grid=(B, H // hp),
        in_specs=specs * 3,
        out_specs=pl.BlockSpec((1, hp * FOX_DH, S), lambda b, h: (b, h, 0)),
        out_shape=jax.ShapeDtypeStruct((B, FOX_W, S), BF16),
        scratch_shapes=(
            per_head((tq, tq), F32)
            + per_head((tq, tq), BF16)
            + per_head((R, tq), F32)
            + per_head((8, tq), F32)
            + per_head((8, tq), F32)
            + per_head((8, tq), F32)
        ),
        compiler_params=pltpu.CompilerParams(
            dimension_semantics=("arbitrary", "arbitrary"),
            vmem_limit_bytes=VMEM_LIMIT),
        name="fox",
    )(*([qf] * hp + [kf] * hp + [vf] * hp))


def _merge_kernel(x_ref, or_ref, of_ref, gmix_ref, wa_ref, wro_ref, wfo_ref, wout_ref, o_ref):
    x = x_ref[0]
    ms = jnp.mean(x * x, axis=-1, keepdims=True)
    h = (x * lax.rsqrt(ms + EPS) * gmix_ref[...]).astype(BF16)
    a = jnp.dot(h, wa_ref[...], preferred_element_type=F32)
    y_r = lax.dot_general(or_ref[0], wro_ref[...], TN_DIMS, preferred_element_type=F32)
    y_f = lax.dot_general(of_ref[0], wfo_ref[...], TN_DIMS, preferred_element_type=F32)
    merged = (jax.nn.sigmoid(a[:, :D_MODEL]) * y_r
              + jax.nn.sigmoid(a[:, D_MODEL:]) * y_f).astype(BF16)
    o_ref[0] = x + jnp.dot(merged, wout_ref[...], preferred_element_type=F32)


def _merge(x, o_r, o_f, gmix, wa, wro, wfo, wout):
    B, S, D = x.shape
    tm = min(TM_MERGE, S)
    const = lambda b, s: (0, 0)
    w = lambda shape: pl.BlockSpec(shape, const, pipeline_mode=pl.Buffered(1))
    return pl.pallas_call(
        _merge_kernel,
        grid=(B, S // tm),
        in_specs=[
            pl.BlockSpec((1, tm, D), lambda b, s: (b, s, 0)),
            pl.BlockSpec((1, RET_V, tm), lambda b, s: (b, 0, s)),
            pl.BlockSpec((1, FOX_W, tm), lambda b, s: (b, 0, s)),
            pl.BlockSpec((1, D), const),
            w((D, 2 * D)), w((RET_V, D)), w((FOX_W, D)), w((D, D)),
        ],
        out_specs=pl.BlockSpec((1, tm, D), lambda b, s: (b, s, 0)),
        out_shape=jax.ShapeDtypeStruct((B, S, D), F32),
        compiler_params=pltpu.CompilerParams(
            dimension_semantics=("arbitrary", "arbitrary"),
            vmem_limit_bytes=VMEM_LIMIT),
        name="merge",
    )(x, o_r, o_f, gmix, wa, wro, wfo, wout)


def _ffn_kernel(x_ref, g_ref, wg_ref, wu_ref, wd_ref, o_ref):
    x = x_ref[0]
    ms = jnp.mean(x * x, axis=-1, keepdims=True)
    h = (x * lax.rsqrt(ms + EPS) * g_ref[...]).astype(BF16)
    gate = jnp.dot(h, wg_ref[...], preferred_element_type=F32)
    up = jnp.dot(h, wu_ref[...], preferred_element_type=F32)
    act = (gate * jax.nn.sigmoid(gate) * up).astype(BF16)
    o_ref[0] = x + jnp.dot(act, wd_ref[...], preferred_element_type=F32)


def _ffn(x, g, wg, wu, wd):
    B, S, D = x.shape
    tm = min(TM_FFN, S)
    const = lambda b, s: (0, 0)
    w = lambda shape: pl.BlockSpec(shape, const, pipeline_mode=pl.Buffered(1))
    return pl.pallas_call(
        _ffn_kernel,
        grid=(B, S // tm),
        in_specs=[
            pl.BlockSpec((1, tm, D), lambda b, s: (b, s, 0)),
            pl.BlockSpec((1, D), const),
            w((D, D_FF)), w((D, D_FF)), w((D_FF, D)),
        ],
        out_specs=pl.BlockSpec((1, tm, D), lambda b, s: (b, s, 0)),
        out_shape=jax.ShapeDtypeStruct((B, S, D), F32),
        compiler_params=pltpu.CompilerParams(
            dimension_semantics=("arbitrary", "arbitrary"),
            vmem_limit_bytes=VMEM_LIMIT),
        name="ffn",
    )(x, g, wg, wu, wd)


def _rope_tables(S):
    half = RET_DK // 2
    pos = jnp.arange(S, dtype=F32)
    inv_freq = 1.0 / (ROPE_BASE ** (jnp.arange(half, dtype=F32) / half))
    ang = pos[:, None] * inv_freq[None, :]
    return jnp.cos(ang).T, jnp.sin(ang).T


def _retention_tables():
    log_g = jnp.log1p(-(2.0 ** (-5.0 - jnp.arange(RET_HEADS, dtype=F32))))
    idx = jnp.arange(CHUNK, dtype=F32)
    diff = idx[:, None] - idx[None, :]
    decay = jnp.where(diff[None] >= 0,
                      jnp.exp(jnp.maximum(diff, 0.0)[None] * log_g[:, None, None]), 0.0)
    decay_t = jnp.swapaxes(decay, 1, 2)
    zeta = jnp.exp((CHUNK - 1.0 - idx)[None, :] * log_g[:, None])[:, None, :]
    xi = jnp.exp((idx + 1.0)[None, :] * log_g[:, None])[:, None, :]
    gch = jnp.exp(CHUNK * log_g)[:, None, None]
    return decay_t, zeta, xi, gch


def kernel(x, g_mix, w_in, b_forget, g_ret_norm, w_ret_o, g_fox_q, g_fox_k, w_fox_o,
           w_out, g_ffn, w_gate, w_up, w_down):
    B, S, D = x.shape
    depth = g_mix.shape[0]
    cos_t, sin_t = _rope_tables(S)
    decay_t, zeta, xi, gch = _retention_tables()
    n_z = Z_FF + FOX_HEADS
    for l in range(depth):
        wt = jnp.pad(w_in[l][:, :n_z].T, ((0, Z_ROWS - n_z), (0, 0))).astype(BF16)
        wa = w_in[l][:, n_z:].astype(BF16)
        gmix = g_mix[l][None, :]
        qr, kr, vr, gr, qf, kf, vf = _inproj(
            x, gmix, wt, cos_t, sin_t,
            g_fox_q[l][:, None], g_fox_k[l][:, None], b_forget[l][:, None])
        o_r = _retention(qr, kr, vr, gr, decay_t, zeta, xi, gch,
                         g_ret_norm[l].reshape(RET_HEADS, RET_DV, 1))
        o_f = _fox(qf, kf, vf)
        x = _merge(x, o_r, o_f, gmix, wa, w_ret_o[l].astype(BF16), w_fox_o[l].astype(BF16),
                   w_out[l].astype(BF16))
        x = _ffn(x, g_ffn[l][None, :], w_gate[l].astype(BF16), w_up[l].astype(BF16),
                 w_down[l].astype(BF16))
    return x
```
